```python
import math
import jax
import jax.numpy as jnp
from jax import lax
import numpy as np

D_MODEL = 2048
BATCH = 16
SEQ = 256
DEPTH = 4
DEC_BATCH = 4
DEC_SEQ = 4096
PAST_LEN = 256

GRID_W = 64
HEAD_DIM = 128
AXIS_DIM = HEAD_DIM // 2
ROPE_THETA = 10000.0
Q_BLOCK = 128
A_HEADS = 8
A_KV_HEADS = 2
B_HEADS = 4
B_VDIM = 2 * HEAD_DIM
C_HEADS = 8
C_KV_HEADS = 2
WINDOW = 128
D_HEADS = 4
D_QK = HEAD_DIM
D_V = 2 * HEAD_DIM
CHUNK = 128
MIX_WIDTH = A_HEADS * HEAD_DIM + B_HEADS * B_VDIM
AB_WIDTH = A_HEADS * HEAD_DIM + 2 * A_KV_HEADS * HEAD_DIM + 2 * B_HEADS * 2 * HEAD_DIM + B_HEADS * B_VDIM
CD_WIDTH = C_HEADS * HEAD_DIM + 2 * C_KV_HEADS * HEAD_DIM + 2 * D_HEADS * D_QK + 2 * D_HEADS * D_V + 4 * D_HEADS
N_EVEN = (DEPTH + 1) // 2
N_ODD = DEPTH // 2
N_EXPERTS = 32
TOP_K = 4
EXPERT_FF = 2048
SWIGLU_LIMIT = 7.0
SWIGLU_ALPHA = 1.702
MOE_BLOCK = 128
EPS = 1e-6
NEG_INF = -1e30

kernel_name = 'hybrid_flow_trunk_step'


def _rmsnorm(x, g):
    x32 = x.astype(jnp.float32)
    y = x32 * lax.rsqrt(jnp.mean(x32 * x32, axis=-1, keepdims=True) + EPS)
    return (y * g.astype(jnp.float32)).astype(x.dtype)


def _modulate(h, shift, scale):
    return h * (1.0 + scale) + shift


def _split(p, sizes):
    out, off = [], 0
    for s in sizes:
        out.append(p[..., off:off + s])
        off += s
    return out


def _heads(x, n, d):
    B, T, _ = x.shape
    return x.reshape(B, T, n, d).transpose(0, 2, 1, 3)


def _merge(o):
    B, H, T, d = o.shape
    return o.transpose(0, 2, 1, 3).reshape(B, T, H * d)


def _rope_tables(n_rows, dtype):
    row = jnp.repeat(jnp.arange(n_rows, dtype=jnp.float32), GRID_W)
    col = jnp.tile(jnp.arange(GRID_W, dtype=jnp.float32), n_rows)
    inv = ROPE_THETA ** (-jnp.arange(0, AXIS_DIM, 2, dtype=jnp.float32) / AXIS_DIM)
    ang = jnp.stack([row[:, None] * inv, col[:, None] * inv], axis=1)
    return jnp.cos(ang).astype(dtype), jnp.sin(ang).astype(dtype)


def _apply_rope(x, cos, sin):
    xr = x.reshape(x.shape[:-1] + (2, 2, AXIS_DIM // 2))
    x1, x2 = xr[..., 0, :], xr[..., 1, :]
    out = jnp.stack([x1 * cos - x2 * sin, x2 * cos + x1 * sin], axis=-2)
    return out.reshape(x.shape)


def _attend_dense(q, k, v, sink=None):
    B, Hq, T, d = q.shape
    Hk = k.shape[1]
    G = Hq // Hk
    dv = v.shape[-1]
    nb = T // Q_BLOCK
    qb = jnp.moveaxis(q.reshape(B, Hk, G, nb, Q_BLOCK, d), 3, 0)

    def block(qblk):
        s = jnp.einsum('bhgqd,bhsd->bhgqs', qblk, k, preferred_element_type=jnp.float32)
        m = jnp.max(s, axis=-1, keepdims=True)
        if sink is not None:
            sk = sink.astype(jnp.float32).reshape(1, Hk, G, 1, 1)
            m = jnp.maximum(m, sk)
            p = jnp.exp(s - m)
            den = jnp.sum(p, axis=-1, keepdims=True) + jnp.exp(sk - m)
        else:
            p = jnp.exp(s - m)
            den = jnp.sum(p, axis=-1, keepdims=True)
        return jnp.einsum('bhgqs,bhsv->bhgqv', (p / den).astype(v.dtype), v)

    o = lax.map(block, qb)
    return jnp.moveaxis(o, 0, 3).reshape(B, Hq, T, dv)


def _windowed_attend(q, k, v, k_ctx, v_ctx, sink):
    B, Hq, T, d = q.shape
    Hk = k.shape[1]
    G = Hq // Hk
    dv = v.shape[-1]
    nb = T // WINDOW
    qb = q.reshape(B, Hk, G, nb, WINDOW, d)

    def band(a):
        ab = a.reshape(B, Hk, nb, WINDOW, a.shape[-1])
        z = jnp.zeros_like(ab[:, :, :1])
        prev = jnp.concatenate([z, ab[:, :, :-1]], axis=2)
        nxt = jnp.concatenate([ab[:, :, 1:], z], axis=2)
        return jnp.concatenate([prev, ab, nxt], axis=3)

    kb, vb = band(k), band(v)
    blk = jnp.arange(nb)
    qpos = blk[:, None] * WINDOW + jnp.arange(WINDOW)[None, :]
    kpos = (blk[:, None] - 1) * WINDOW + jnp.arange(3 * WINDOW)[None, :]
    ok = (jnp.abs(qpos[:, :, None] - kpos[:, None, :]) <= WINDOW) & (kpos[:, None, :] >= 0) & (kpos[:, None, :] < T)
    s_loc = jnp.where(ok, jnp.einsum('bhgnqd,bhnkd->bhgnqk', qb, kb, preferred_element_type=jnp.float32), NEG_INF)
    s_ctx = jnp.einsum('bhgnqd,bhsd->bhgnqs', qb, k_ctx, preferred_element_type=jnp.float32)
    sk = sink.astype(jnp.float32).reshape(1, Hk, G, 1, 1, 1)
    m = jnp.maximum(jnp.maximum(jnp.max(s_loc, -1, keepdims=True), jnp.max(s_ctx, -1, keepdims=True)), sk)
    p_loc = jnp.exp(s_loc - m)
    p_ctx = jnp.exp(s_ctx - m)
    den = jnp.sum(p_loc, -1, keepdims=True) + jnp.sum(p_ctx, -1, keepdims=True) + jnp.exp(sk - m)
    o = (jnp.einsum('bhgnqk,bhnkv->bhgnqv', (p_loc / den).astype(v.dtype), vb)
         + jnp.einsum('bhgnqs,bhsv->bhgnqv', (p_ctx / den).astype(v.dtype), v_ctx))
    return o.reshape(B, Hq, T, dv)


def _mlstm_scan(q, k, v, ig, lf, C0, n0, m0):
    B, H, T, _ = q.shape
    dv = v.shape[-1]
    nc = T // CHUNK

    def chunks(a):
        a = a.astype(jnp.float32).reshape((B, H, nc, CHUNK) + a.shape[3:])
        return jnp.moveaxis(a, 2, 0)

    tril = jnp.tril(jnp.ones((CHUNK, CHUNK), dtype=bool))

    def step(carry, inp):
        C, n, m = carry
        qj, kj, vj, ij, fj = inp
        b = jnp.cumsum(fj, axis=-1)
        dmat = jnp.where(tril, b[..., :, None] - b[..., None, :] + ij[..., None, :], NEG_INF)
        m_prev = b + m[..., None]
        m_t = jnp.maximum(m_prev, jnp.max(dmat, axis=-1))
        w_intra = jnp.exp(dmat - m_t[..., None])
        w_prev = jnp.exp(m_prev - m_t)
        s = jnp.einsum('bhld,bhsd->bhls', qj, kj) * w_intra
        num = w_prev[..., None] * jnp.einsum('bhld,bhdv->bhlv', qj, C) + jnp.einsum('bhls,bhsv->bhlv', s, vj)
        den = w_prev * jnp.einsum('bhld,bhd->bhl', qj, n) + jnp.sum(s, axis=-1)
        h = num / jnp.maximum(jnp.abs(den), jnp.exp(-m_t))[..., None]
        b_end = b[..., -1]
        g = b_end[..., None] - b + ij
        m_new = jnp.maximum(b_end + m, jnp.max(g, axis=-1))
        wk = jnp.exp(g - m_new[..., None])
        decay = jnp.exp(b_end + m - m_new)
        C_new = decay[..., None, None] * C + jnp.einsum('bhl,bhld,bhlv->bhdv', wk, kj, vj)
        n_new = decay[..., None] * n + jnp.einsum('bhl,bhld->bhd', wk, kj)
        return (C_new, n_new, m_new), h

    carry0 = (C0.astype(jnp.float32), n0.astype(jnp.float32), m0.astype(jnp.float32))
    (C, n, m), hs = lax.scan(step, carry0, (chunks(q), chunks(k), chunks(v), chunks(ig), chunks(lf)))
    h = jnp.moveaxis(hs, 0, 2).reshape(B, H, T, dv)
    return h, C, n, m


def _mlstm_bidir(q, k, v, ig, lf, C0, n0, m0):
    flip = lambda a: jnp.flip(a, axis=2)
    h_f, C_f, n_f, m_f = _mlstm_scan(q, k, v, ig[:, 0], lf[:, 0], C0[:, 0], n0[:, 0], m0[:, 0])
    h_b, C_b, n_b, m_b = _mlstm_scan(flip(q), flip(k), flip(v), flip(ig[:, 1]), flip(lf[:, 1]), C0[:, 1], n0[:, 1], m0[:, 1])
    return h_f + flip(h_b), jnp.stack([C_f, C_b], 1), jnp.stack([n_f, n_b], 1), jnp.stack([m_f, m_b], 1)


def _ab_project(h, w_in, a_qg, a_kg):
    B, T, _ = h.shape
    p = jnp.einsum('btd,de->bte', h, w_in)
    aq, ak, av, bq, bk, bv = _split(p, [A_HEADS * HEAD_DIM, A_KV_HEADS * HEAD_DIM, A_KV_HEADS * HEAD_DIM,
                                       B_HEADS * 2 * HEAD_DIM, B_HEADS * 2 * HEAD_DIM, B_HEADS * B_VDIM])
    aq = _rmsnorm(_heads(aq, A_HEADS, HEAD_DIM), a_qg)
    ak = _rmsnorm(_heads(ak, A_KV_HEADS, HEAD_DIM), a_kg)
    av = _heads(av, A_KV_HEADS, HEAD_DIM)
    bq = bq.reshape(B, T, B_HEADS, 2, HEAD_DIM).transpose(0, 3, 2, 1, 4)
    bk = bk.reshape(B, T, B_HEADS, 2, HEAD_DIM).transpose(0, 3, 2, 1, 4)
    bv = _heads(bv, B_HEADS, B_VDIM)
    return aq, ak, av, bq, bk, bv


def _ab_mix(aq, ak, av, bq, bk, bv, b_lam, b_subg, lam_init):
    scale = HEAD_DIM ** -0.5
    oa = _attend_dense(aq * scale, ak, av)
    o1 = _attend_dense(bq[:, 0] * scale, bk[:, 0], bv)
    o2 = _attend_dense(bq[:, 1] * scale, bk[:, 1], bv)
    lam32 = b_lam.astype(jnp.float32)
    lam = jnp.exp(jnp.sum(lam32[0] * lam32[1])) - jnp.exp(jnp.sum(lam32[2] * lam32[3])) + lam_init
    ob = o1.astype(jnp.float32) - lam * o2.astype(jnp.float32)
    ob = (_rmsnorm(ob, b_subg) * (1.0 - lam_init)).astype(oa.dtype)
    return jnp.concatenate([_merge(oa), _merge(ob)], axis=-1)


def _cd_project(h, w_in, b_gates):
    B, T, _ = h.shape
    p = jnp.einsum('btd,de->bte', h, w_in)
    cq, ck, cv, dq, dk, dv, do, dg = _split(p, [C_HEADS * HEAD_DIM, C_KV_HEADS * HEAD_DIM, C_KV_HEADS * HEAD_DIM,
                                               D_HEADS * D_QK, D_HEADS * D_QK, D_HEADS * D_V, D_HEADS * D_V, 4 * D_HEADS])
    cq = _heads(cq, C_HEADS, HEAD_DIM)
    ck = _heads(ck, C_KV_HEADS, HEAD_DIM)
    cv = _heads(cv, C_KV_HEADS, HEAD_DIM)
    dq = _heads(dq, D_HEADS, D_QK)
    dk = _heads(dk, D_HEADS, D_QK) * (D_QK ** -0.5)
    dv = _heads(dv, D_HEADS, D_V)
    dg = (dg.astype(jnp.float32) + b_gates.astype(jnp.float32)).reshape(B, T, 2, 2, D_HEADS).transpose(0, 2, 3, 4, 1)
    ig = dg[:, :, 0]
    lf = jax.nn.log_sigmoid(dg[:, :, 1])
    return cq, ck, cv, dq, dk, dv, do, ig, lf


def _cd_out(oc, hd, do, d_ng):
    B, _, T, _ = hd.shape
    hn = _rmsnorm(hd, d_ng.reshape(D_HEADS, 1, D_V))
    od = hn.transpose(0, 2, 1, 3).reshape(B, T, D_HEADS * D_V).astype(do.dtype) * jax.nn.sigmoid(do)
    return jnp.concatenate([_merge(oc), od], axis=-1)


def _moe(x, router_w, router_b, w_gu, b_gu, w_dn, b_dn):
    N, D = x.shape
    NK = N * TOP_K
    logits = jnp.einsum('nd,de->ne', x, router_w, preferred_element_type=jnp.float32) + router_b.astype(jnp.float32)
    top_val, top_idx = lax.top_k(logits, TOP_K)
    gate = jax.nn.softmax(top_val, axis=-1)
    e_flat = top_idx.reshape(-1)
    tok_flat = jnp.arange(NK, dtype=jnp.int32) // TOP_K
    g_flat = gate.reshape(-1)
    order = jnp.argsort(e_flat)
    e_s, tok_s, g_s = e_flat[order], tok_flat[order], g_flat[order]
    counts = jnp.bincount(e_flat, length=N_EXPERTS)
    starts = jnp.cumsum(counts) - counts
    padded = (counts + MOE_BLOCK - 1) // MOE_BLOCK * MOE_BLOCK
    pad_end = jnp.cumsum(padded)
    pad_start = pad_end - padded
    dest = pad_start[e_s] + (jnp.arange(NK, dtype=jnp.int32) - starts[e_s])
    n_blocks = (NK + MOE_BLOCK - 1) // MOE_BLOCK + N_EXPERTS
    P = n_blocks * MOE_BLOCK
    buf_tok = jnp.full((P,), N, dtype=jnp.int32).at[dest].set(tok_s)
    buf_gate = jnp.zeros((P,), jnp.float32).at[dest].set(g_s)
    blk_expert = jnp.minimum(jnp.searchsorted(pad_end, jnp.arange(n_blocks, dtype=jnp.int32) * MOE_BLOCK, side='right'),
                             N_EXPERTS - 1)
    x_pad = jnp.concatenate([x, jnp.zeros((1, D), x.dtype)], axis=0)
    xb = x_pad[buf_tok].reshape(n_blocks, MOE_BLOCK, D)

    def expert_block(args):
        xblk, e = args
        hgu = jnp.dot(xblk, w_gu[e]) + b_gu[e]
        gt, up = hgu[:, :EXPERT_FF], hgu[:, EXPERT_FF:]
        gt = jnp.minimum(gt, SWIGLU_LIMIT)
        up = jnp.clip(up, -SWIGLU_LIMIT, SWIGLU_LIMIT)
        act = gt * jax.nn.sigmoid(SWIGLU_ALPHA * gt) * (up + 1.0)
        return jnp.dot(act, w_dn[e]) + b_dn[e]

    yb = lax.map(expert_block, (xb, blk_expert)).reshape(P, D)
    y = jax.ops.segment_sum(yb * buf_gate[:, None].astype(yb.dtype), buf_tok, num_segments=N + 1)
    return y[:N]


def setup_inputs(seed: int = 0) -> dict:
    key = jax.random.key(seed)
    ks = iter(jax.random.split(key, 48))
    D = D_MODEL

    def nrm(shape, s=1.0):
        return s * jax.random.normal(next(ks), shape, jnp.float32)

    i_bias = nrm((N_ODD, 2, 1, D_HEADS), 0.1)
    f_bias = 3.0 + nrm((N_ODD, 2, 1, D_HEADS), 0.5)
    b_gates = jnp.concatenate([i_bias, f_bias], axis=2).reshape(N_ODD, 4 * D_HEADS)
    return {
        'x_prompt': nrm((BATCH, SEQ, D)),
        'x_sample': nrm((DEC_BATCH, DEC_SEQ, D)),
        'cache_a_k': nrm((DEC_BATCH, N_EVEN, A_KV_HEADS, PAST_LEN, HEAD_DIM)),
        'cache_a_v': nrm((DEC_BATCH, N_EVEN, A_KV_HEADS, PAST_LEN, HEAD_DIM)),
        'cache_b_k': nrm((DEC_BATCH, N_EVEN, 2, B_HEADS, PAST_LEN, HEAD_DIM)),
        'cache_b_v': nrm((DEC_BATCH, N_EVEN, B_HEADS, PAST_LEN, B_VDIM)),
        'cache_c_k': nrm((DEC_BATCH, N_ODD, C_KV_HEADS, PAST_LEN, HEAD_DIM)),
        'cache_c_v': nrm((DEC_BATCH, N_ODD, C_KV_HEADS, PAST_LEN, HEAD_DIM)),
        'state_d_C': nrm((DEC_BATCH, N_ODD, 2, D_HEADS, D_QK, D_V), 0.1),
        'state_d_n': nrm((DEC_BATCH, N_ODD, 2, D_HEADS, D_QK), 0.1),
        'state_d_m': nrm((DEC_BATCH, N_ODD, 2, D_HEADS), 0.5),
        'c': nrm((DEC_BATCH, D)),
        'c_ctx': nrm((D,)),
        'norm_mix_g': 1.0 + nrm((DEPTH, D), 0.1),
        'norm_ffn_g': 1.0 + nrm((DEPTH, D), 0.1),
        'w_mod': nrm((DEPTH, D, 6 * D), 0.5 * D ** -0.5),
        'b_mod': nrm((DEPTH, 6 * D), 0.02),
        'w_out': nrm((DEPTH, MIX_WIDTH, D), MIX_WIDTH ** -0.5),
        'w_in_ab': nrm((N_EVEN, D, AB_WIDTH), D ** -0.5),
        'a_q_norm_g': 1.0 + nrm((N_EVEN, HEAD_DIM), 0.1),
        'a_k_norm_g': 1.0 + nrm((N_EVEN, HEAD_DIM), 0.1),
        'b_lambda': nrm((N_EVEN, 4, HEAD_DIM), 0.1),
        'b_subln_g': 1.0 + nrm((N_EVEN, B_VDIM), 0.1),
        'w_in_cd': nrm((N_ODD, D, CD_WIDTH), D ** -0.5),
        'b_gates': b_gates,
        'c_sink': nrm((N_ODD, C_HEADS), 0.5),
        'd_norm_g': 1.0 + nrm((N_ODD, D_HEADS * D_V), 0.1),
        'router_w': nrm((DEPTH, D, N_EXPERTS), D ** -0.5),
        'router_b': nrm((DEPTH, N_EXPERTS), 0.01),
        'w_gate_up': nrm((DEPTH, N_EXPERTS, D, 2 * EXPERT_FF), D ** -0.5),
        'b_gate_up': nrm((DEPTH, N_EXPERTS, 2 * EXPERT_FF), 0.02),
        'w_down': nrm((DEPTH, N_EXPERTS, EXPERT_FF, D), EXPERT_FF ** -0.5),
        'b_down': nrm((DEPTH, N_EXPERTS, D), 0.02),
        'final_norm_g': 1.0 + nrm((D,), 0.1),
    }


def reference(x_prompt, x_sample, cache_a_k, cache_a_v, cache_b_k, cache_b_v, cache_c_k, cache_c_v,
              state_d_C, state_d_n, state_d_m, c, c_ctx, norm_mix_g, norm_ffn_g, w_mod, b_mod, w_out,
              w_in_ab, a_q_norm_g, a_k_norm_g, b_lambda, b_subln_g, w_in_cd, b_gates, c_sink, d_norm_g,
              router_w, router_b, w_gate_up, b_gate_up, w_down, b_down, final_norm_g):
    xp, xs = x_prompt, x_sample
    Bp, Tp, D = xp.shape
    Bs, Ts, _ = xs.shape
    n_rows = Ts // GRID_W
    cos, sin = _rope_tables(n_rows, xs.dtype)
    scale = HEAD_DIM ** -0.5
    silu_ctx = jax.nn.silu(c_ctx)
    silu_c = jax.nn.silu(c)
    ctx_a_k, ctx_a_v, ctx_b_k, ctx_b_v = [], [], [], []
    ctx_c_k, ctx_c_v, ctx_d_C, ctx_d_n, ctx_d_m = [], [], [], [], []

    for layer in range(DEPTH):
        j = layer // 2
        mod_p = (jnp.dot(silu_ctx, w_mod[layer]) + b_mod[layer])[None, None, :]
        mod_s = (jnp.dot(silu_c, w_mod[layer]) + b_mod[layer])[:, None, :]
        sh1_p, sc1_p, g1_p, sh2_p, sc2_p, g2_p = jnp.split(mod_p, 6, axis=-1)
        sh1_s, sc1_s, g1_s, sh2_s, sc2_s, g2_s = jnp.split(mod_s, 6, axis=-1)
        h_p = _modulate(_rmsnorm(xp, norm_mix_g[layer]), sh1_p, sc1_p)
        h_s = _modulate(_rmsnorm(xs, norm_mix_g[layer]), sh1_s, sc1_s)

        if layer % 2 == 0:
            lam_init = 0.8 - 0.6 * math.exp(-0.3 * layer)
            aq, ak, av, bq, bk, bv = _ab_project(h_p, w_in_ab[j], a_q_norm_g[j], a_k_norm_g[j])
            ctx_a_k.append(ak)
            ctx_a_v.append(av)
            ctx_b_k.append(bk)
            ctx_b_v.append(bv)
            o_p = _ab_mix(aq, ak, av, bq, bk, bv, b_lambda[j], b_subln_g[j], lam_init)
            aq, ak, av, bq, bk, bv = _ab_project(h_s, w_in_ab[j], a_q_norm_g[j], a_k_norm_g[j])
            ak = jnp.concatenate([_apply_rope(ak, cos, sin), cache_a_k[:, j]], axis=2)
            av = jnp.concatenate([av, cache_a_v[:, j]], axis=2)
            bk = jnp.concatenate([_apply_rope(bk, cos, sin), cache_b_k[:, j]], axis=3)
            bv = jnp.concatenate([bv, cache_b_v[:, j]], axis=2)
            o_s = _ab_mix(_apply_rope(aq, cos, sin), ak, av, _apply_rope(bq, cos, sin), bk, bv,
                          b_lambda[j], b_subln_g[j], lam_init)
        else:
            cq, ck, cv, dq, dk, dv, do, ig, lf = _cd_project(h_p, w_in_cd[j], b_gates[j])
            ctx_c_k.append(ck)
            ctx_c_v.append(cv)
            oc = _attend_dense(cq * scale, ck, cv, c_sink[j])
            C0 = jnp.zeros((Bp, 2, D_HEADS, D_QK, D_V), jnp.float32)
            n0 = jnp.zeros((Bp, 2, D_HEADS, D_QK), jnp.float32)
            m0 = jnp.zeros((Bp, 2, D_HEADS), jnp.float32)
            hd, dC, dn, dm = _mlstm_bidir(dq, dk, dv, ig, lf, C0, n0, m0)
            ctx_d_C.append(dC)
            ctx_d_n.append(dn)
            ctx_d_m.append(dm)
            o_p = _cd_out(oc, hd, do, d_norm_g[j])
            cq, ck, cv, dq, dk, dv, do, ig, lf = _cd_project(h_s, w_in_cd[j], b_gates[j])
            oc = _windowed_attend(_apply_rope(cq, cos, sin) * scale, _apply_rope(ck, cos, sin), cv,
                                  cache_c_k[:, j], cache_c_v[:, j], c_sink[j])
            hd, _, _, _ = _mlstm_bidir(dq, dk, dv, ig, lf, state_d_C[:, j], state_d_n[:, j], state_d_m[:, j])
            o_s = _cd_out(oc, hd, do, d_norm_g[j])

        xp = xp + g1_p * jnp.einsum('btm,md->btd', o_p, w_out[layer])
        xs = xs + g1_s * jnp.einsum('btm,md->btd', o_s, w_out[layer])

        f_p = _modulate(_rmsnorm(xp, norm_ffn_g[layer]), sh2_p, sc2_p)
        f_s = _modulate(_rmsnorm(xs, norm_ffn_g[layer]), sh2_s, sc2_s)
        tokens = jnp.concatenate([f_p.reshape(Bp * Tp, D), f_s.reshape(Bs * Ts, D)], axis=0)
        y = _moe(tokens, router_w[layer], router_b[layer], w_gate_up[layer], b_gate_up[layer],
                 w_down[layer], b_down[layer])
        xp = xp + g2_p * y[:Bp * Tp].reshape(Bp, Tp, D)
        xs = xs + g2_s * y[Bp * Tp:].reshape(Bs, Ts, D)

    y_prompt = _rmsnorm(xp, final_norm_g)
    y_sample = _rmsnorm(xs, final_norm_g)
    new_a_k = jnp.stack(ctx_a_k, axis=1)
    new_a_v = jnp.stack(ctx_a_v, axis=1)
    new_b_k = jnp.stack(ctx_b_k, axis=1)
    new_b_v = jnp.stack(ctx_b_v, axis=1)
    new_c_k = jnp.stack(ctx_c_k, axis=1)
    new_c_v = jnp.stack(ctx_c_v, axis=1)
    new_d_C = jnp.stack(ctx_d_C, axis=1)
    new_d_n = jnp.stack(ctx_d_n, axis=1)
    new_d_m = jnp.stack(ctx_d_m, axis=1)
    return (y_prompt, y_sample, new_a_k, new_a_v, new_b_k, new_b_v, new_c_k, new_c_v, new_d_C, new_d_n, new_d_m)
```

```python
import functools
import math

import jax
import jax.numpy as jnp
from jax import lax
from jax.experimental import pallas as pl
from jax.experimental.pallas import tpu as pltpu

HEAD_DIM = 128
GRID_W = 64
AXIS_DIM = HEAD_DIM // 2
ROPE_THETA = 10000.0
A_HEADS = 8
A_KV_HEADS = 2
B_HEADS = 4
B_VDIM = 2 * HEAD_DIM
C_HEADS = 8
C_KV_HEADS = 2
WINDOW = 128
D_HEADS = 4
D_QK = HEAD_DIM
D_V = 2 * HEAD_DIM
CHUNK = 128
MIX_WIDTH = A_HEADS * HEAD_DIM + B_HEADS * B_VDIM
AB_WIDTH = A_HEADS * HEAD_DIM + 2 * A_KV_HEADS * HEAD_DIM + 2 * B_HEADS * 2 * HEAD_DIM + B_HEADS * B_VDIM
CD_MAIN = C_HEADS * HEAD_DIM + 2 * C_KV_HEADS * HEAD_DIM + 2 * D_HEADS * D_QK + 2 * D_HEADS * D_V
N_GATES = 4 * D_HEADS
TOP_K = 4
SWIGLU_LIMIT = 7.0
SWIGLU_ALPHA = 1.702
EPS = 1e-6
NEG_INF = -1e30
LANES = 128
MOE_TILE = 1024
MOE_SUB = 256
VMEM_LIMIT = 56 * 1024 * 1024

_MXU = jnp.bfloat16


def _dot(a, b):
    return jnp.dot(a, b, preferred_element_type=jnp.float32)


def _dot_nt(a, b):
    return lax.dot_general(a, b, (((1,), (1,)), ((), ())), preferred_element_type=jnp.float32)


def _dot_tn(a, b):
    return lax.dot_general(a, b, (((0,), (0,)), ((), ())), preferred_element_type=jnp.float32)


def _split_hi_lo(a):
    hi = a.astype(_MXU)
    lo = (a - hi.astype(jnp.float32)).astype(_MXU)
    return hi, lo


def _sigmoid(x):
    return 1.0 / (1.0 + jnp.exp(-x))


def _log_sigmoid(x):
    return -(jnp.maximum(-x, 0.0) + jnp.log1p(jnp.exp(-jnp.abs(x))))


def _tile(n, cap):
    if n <= cap:
        return n
    t = cap - cap % LANES
    while n % t:
        t -= LANES
    return t


def _params(*sem):
    return pltpu.CompilerParams(dimension_semantics=sem, vmem_limit_bytes=VMEM_LIMIT)


def _mod_kernel(s_ref, w_ref, b_ref, o_ref):
    s = s_ref[...]
    s = s * _sigmoid(s)
    o_ref[...] = _dot(s.astype(_MXU), w_ref[...].astype(_MXU)) + b_ref[...]


def _modulation(cond_rows, w_mod, b_mod):
    depth, d, n6 = w_mod.shape
    rows = cond_rows.shape[0]
    tn = _tile(n6, 1024)
    return pl.pallas_call(
        _mod_kernel,
        out_shape=jax.ShapeDtypeStruct((depth, rows, n6), jnp.float32),
        grid=(depth, n6 // tn),
        in_specs=[
            pl.BlockSpec((rows, d), lambda l, j: (0, 0)),
            pl.BlockSpec((None, d, tn), lambda l, j: (l, 0, j)),
            pl.BlockSpec((None, 1, tn), lambda l, j: (l, 0, j)),
        ],
        out_specs=pl.BlockSpec((None, rows, tn), lambda l, j: (l, 0, j)),
        compiler_params=_params("arbitrary", "arbitrary"),
        name="modulation",
    )(cond_rows, w_mod, b_mod.reshape(depth, 1, n6))


def _rope(y, cos, sin_signed):
    lane = lax.broadcasted_iota(jnp.int32, y.shape, 1)
    first = (lane % (2 * (AXIS_DIM // 2))) < (AXIS_DIM // 2)
    partner = jnp.where(first, pltpu.roll(y, HEAD_DIM - AXIS_DIM // 2, 1), pltpu.roll(y, AXIS_DIM // 2, 1))
    return y * cos + partner * sin_signed


def _head_rms(y, g):
    return y * lax.rsqrt(jnp.mean(y * y, axis=-1, keepdims=True) + EPS) * g


def _proj_kernel(x_ref, g_ref, sh_ref, sc_ref, w_ref, cos_ref, sin_ref, qg_ref, kg_ref, *rest,
                 runs, tn, with_gates):
    if with_gates:
        wg_ref, bg_ref, o_ref, og_ref, h_sc = rest
    else:
        o_ref, h_sc = rest
    j = pl.program_id(2)

    @pl.when(j == 0)
    def _():
        x = x_ref[...]
        y = x * lax.rsqrt(jnp.mean(x * x, axis=-1, keepdims=True) + EPS) * g_ref[...]
        h = y * (1.0 + sc_ref[...]) + sh_ref[...]
        h_sc[...] = h.astype(_MXU)
        if with_gates:
            og_ref[...] = _dot(h_sc[...], wg_ref[...].astype(_MXU)) + bg_ref[...]

    acc = _dot(h_sc[...], w_ref[...].astype(_MXU))

    for lo, hi, ops in runs:
        @pl.when((j >= lo) & (j < hi))
        def _(ops=ops):
            for t, (norm, rope, scale) in enumerate(ops):
                y = acc[:, t * HEAD_DIM:(t + 1) * HEAD_DIM]
                if norm == "q":
                    y = _head_rms(y, qg_ref[...])
                elif norm == "k":
                    y = _head_rms(y, kg_ref[...])
                if rope:
                    y = _rope(y, cos_ref[...], sin_ref[...])
                if scale != 1.0:
                    y = y * scale
                o_ref[:, t * HEAD_DIM:(t + 1) * HEAD_DIM] = y


def _tile_runs(head_ops, tn):
    per = tn // HEAD_DIM
    tiles = [tuple(head_ops[i * per:(i + 1) * per]) for i in range(len(head_ops) // per)]
    runs, lo = [], 0
    for i in range(1, len(tiles) + 1):
        if i == len(tiles) or tiles[i] != tiles[lo]:
            runs.append((lo, i, tiles[lo]))
            lo = i
    return tuple(runs)


def _project(x, mods, layer, norm_g, w_in, jl, n_main, head_ops, cos_t, sin_t, qg, kg, gates=None):
    G, T, D = x.shape
    tm = min(512, T)
    tn = _tile(n_main, 512)
    runs = _tile_runs(head_ops, tn)
    with_gates = gates is not None
    in_specs = [
        pl.BlockSpec((None, tm, D), lambda b, i, j: (b, i, 0)),
        pl.BlockSpec((None, 1, D), lambda b, i, j: (layer, 0, 0)),
        pl.BlockSpec((None, None, None, 1, D), lambda b, i, j: (layer, 0, b, 0, 0)),
        pl.BlockSpec((None, None, None, 1, D), lambda b, i, j: (layer, 1, b, 0, 0)),
        pl.BlockSpec((None, D, tn), lambda b, i, j: (jl, 0, j)),
        pl.BlockSpec((None, tm, HEAD_DIM), lambda b, i, j: (jnp.minimum(b, 1), i, 0)),
        pl.BlockSpec((None, tm, HEAD_DIM), lambda b, i, j: (jnp.minimum(b, 1), i, 0)),
        pl.BlockSpec((1, HEAD_DIM), lambda b, i, j: (0, 0)),
        pl.BlockSpec((1, HEAD_DIM), lambda b, i, j: (0, 0)),
    ]
    args = [x, norm_g.reshape(norm_g.shape[0], 1, D), mods, mods, w_in, cos_t, sin_t, qg, kg]
    out_shape = [jax.ShapeDtypeStruct((G, T, n_main), jnp.float32)]
    out_specs = [pl.BlockSpec((None, tm, tn), lambda b, i, j: (b, i, j))]
    if with_gates:
        wg, bg = gates
        in_specs += [pl.BlockSpec((D, LANES), lambda b, i, j: (0, 0)),
                     pl.BlockSpec((1, LANES), lambda b, i, j: (0, 0))]
        args += [wg, bg]
        out_shape.append(jax.ShapeDtypeStruct((G, T, LANES), jnp.float32))
        out_specs.append(pl.BlockSpec((None, tm, LANES), lambda b, i, j: (b, i, 0)))
    res = pl.pallas_call(
        functools.partial(_proj_kernel, runs=runs, tn=tn, with_gates=with_gates),
        out_shape=out_shape,
        grid=(G, T // tm, n_main // tn),
        in_specs=in_specs,
        out_specs=out_specs,
        scratch_shapes=[pltpu.VMEM((tm, D), _MXU)],
        compiler_params=_params("arbitrary", "arbitrary", "arbitrary"),
        name="norm_mod_project",
    )(*args)
    return res if with_gates else res[0]


def _flash_kernel(*refs, groups, tq, tk, n_lat, dv, has_cache, has_sink, diff, lam_init):
    it = iter(refs)
    q_ref, k_ref, v_ref = next(it), next(it), next(it)
    ck_refs = [next(it) for _ in groups] if has_cache else []
    cv_ref = next(it) if has_cache else None
    sink_ref = next(it) if has_sink else None
    if diff:
        lam_ref, subg_ref = next(it), next(it)
    o_ref = next(it)
    scr = [(next(it), next(it), next(it)) for _ in groups]
    kvh = pl.program_id(1)

    qs = []
    for gi, (q_offs, k_off) in enumerate(groups):
        q = jnp.concatenate([q_ref[:, o:o + HEAD_DIM] for o in q_offs], axis=0).astype(_MXU)
        qs.append(q)
        m_sc, l_sc, acc_sc = scr[gi]
        if has_sink:
            n_h = len(q_offs)
            sk = jnp.concatenate(
                [jnp.full((tq, 1), sink_ref[kvh * n_h + t], jnp.float32) for t in range(n_h)], axis=0)
            m_sc[...] = sk
            l_sc[...] = jnp.ones_like(sk)
        else:
            m_sc[...] = jnp.full(m_sc.shape, NEG_INF, jnp.float32)
            l_sc[...] = jnp.zeros(l_sc.shape, jnp.float32)
        acc_sc[...] = jnp.zeros(acc_sc.shape, jnp.float32)

    def update(gi, k, v):
        m_sc, l_sc, acc_sc = scr[gi]
        s = _dot_nt(qs[gi], k)
        m_prev = m_sc[...]
        m_new = jnp.maximum(m_prev, jnp.max(s, axis=-1, keepdims=True))
        alpha = jnp.exp(m_prev - m_new)
        p = jnp.exp(s - m_new)
        l_sc[...] = alpha * l_sc[...] + jnp.sum(p, axis=-1, keepdims=True)
        acc_sc[...] = alpha * acc_sc[...] + _dot(p.astype(_MXU), v)
        m_sc[...] = m_new

    def chunk(c, carry):
        r0 = pl.multiple_of(c * tk, tk)
        v = v_ref[pl.ds(r0, tk), :].astype(_MXU)
        for gi, (q_offs, k_off) in enumerate(groups):
            k = k_ref[pl.ds(r0, tk), k_off:k_off + HEAD_DIM].astype(_MXU)
            update(gi, k, v)
        return carry

    lax.fori_loop(0, n_lat // tk, chunk, 0)
    if has_cache:
        v = cv_ref[...].astype(_MXU)
        for gi in range(len(groups)):
            update(gi, ck_refs[gi][...].astype(_MXU), v)

    outs = []
    for gi in range(len(groups)):
        m_sc, l_sc, acc_sc = scr[gi]
        outs.append(acc_sc[...] / l_sc[...])
    if diff:
        l32 = lam_ref[...]
        lam = (jnp.exp(jnp.sum(l32[0:1] * l32[1:2], axis=-1, keepdims=True))
               - jnp.exp(jnp.sum(l32[2:3] * l32[3:4], axis=-1, keepdims=True)) + lam_init)
        ob = outs[0] - lam * outs[1]
        ob = ob * lax.rsqrt(jnp.mean(ob * ob, axis=-1, keepdims=True) + EPS) * subg_ref[...]
        o_ref[...] = (ob * (1.0 - lam_init)).astype(o_ref.dtype)
    else:
        o = outs[0]
        for t in range(len(groups[0][0])):
            o_ref[:, t * dv:(t + 1) * dv] = o[t * tq:(t + 1) * tq].astype(o_ref.dtype)


def _attention(p_arr, o_arr, *, n_seq, seq_len, b_off, n_kvh, groups, q_col, q_w, k_col, k_w, v_col, dv,
               o_col, o_w, tq, cache=None, sink=None, diff=None, name):
    G, T, _ = p_arr.shape
    per_b = T // seq_len
    nq = seq_len // tq
    tk = min(512, seq_len)

    def row_map(s, i, blocks_per_seq):
        return b_off + s // per_b, (s % per_b) * blocks_per_seq + i

    def q_map(s, h, i):
        b, r = row_map(s, i, nq)
        return b, r, q_col // q_w + h

    def k_map(s, h, i):
        b, r = row_map(s, 0, 1)
        return b, r, k_col // k_w + h

    def v_map(s, h, i):
        b, r = row_map(s, 0, 1)
        return b, r, v_col // dv + h

    def o_map(s, h, i):
        b, r = row_map(s, i, nq)
        return b, r, o_col // o_w + h

    in_specs = [pl.BlockSpec((None, tq, q_w), q_map),
                pl.BlockSpec((None, seq_len, k_w), k_map),
                pl.BlockSpec((None, seq_len, dv), v_map)]
    args = [p_arr, p_arr, p_arr]
    if cache is not None:
        ck_list, cv = cache
        for arr, spec in ck_list:
            in_specs.append(spec)
            args.append(arr)
        in_specs.append(cv[1])
        args.append(cv[0])
    if sink is not None:
        in_specs.append(pl.BlockSpec(memory_space=pltpu.SMEM))
        args.append(sink)
    lam_init = 0.0
    if diff is not None:
        lam_arr, subg, lam_init = diff
        in_specs += [pl.BlockSpec((4, HEAD_DIM), lambda s, h, i: (0, 0)),
                     pl.BlockSpec((1, dv), lambda s, h, i: (0, 0))]
        args += [lam_arr, subg]
    n_in = len(args)
    in_specs.append(pl.BlockSpec((None, tq, o_w), o_map))
    args.append(o_arr)
    scratch = []
    for q_offs, _ in groups:
        m = len(q_offs) * tq
        scratch += [pltpu.VMEM((m, 1), jnp.float32), pltpu.VMEM((m, 1), jnp.float32),
                    pltpu.VMEM((m, dv), jnp.float32)]

    def body(*refs):
        refs = refs[:n_in] + refs[n_in + 1:]
        _flash_kernel(*refs, groups=groups, tq=tq, tk=tk, n_lat=seq_len, dv=dv,
                      has_cache=cache is not None, has_sink=sink is not None,
                      diff=diff is not None, lam_init=lam_init)

    return pl.pallas_call(
        body,
        out_shape=jax.ShapeDtypeStruct(o_arr.shape, o_arr.dtype),
        grid=(n_seq, n_kvh, nq),
        in_specs=in_specs,
        out_specs=pl.BlockSpec((None, tq, o_w), o_map),
        scratch_shapes=scratch,
        input_output_aliases={n_in: 0},
        compiler_params=_params("arbitrary", "arbitrary", "arbitrary"),
        name=name,
    )(*args)


def _window_kernel(q_ref, k_ref, v_ref, ck_ref, cv_ref, sink_ref, o_ref, *, tq, n_h, seq_len):
    kvh = pl.program_id(1)
    i = pl.program_id(2)
    span = tq + 2 * WINDOW
    start = jnp.clip(i * tq - WINDOW, 0, seq_len - span)
    start = pl.multiple_of(start, WINDOW)
    k = k_ref[pl.ds(start, span), :].astype(_MXU)
    v = v_ref[pl.ds(start, span), :].astype(_MXU)
    q = jnp.concatenate([q_ref[:, t * HEAD_DIM:(t + 1) * HEAD_DIM] for t in range(n_h)], axis=0).astype(_MXU)
    m_rows = n_h * tq
    s_loc = _dot_nt(q, k)
    row = lax.broadcasted_iota(jnp.int32, (m_rows, span), 0)
    col = lax.broadcasted_iota(jnp.int32, (m_rows, span), 1)
    qpos = i * tq + (row & (tq - 1))
    kpos = start + col
    s_loc = jnp.where(jnp.abs(qpos - kpos) <= WINDOW, s_loc, NEG_INF)
    s_ctx = _dot_nt(q, ck_ref[...].astype(_MXU))
    sk = jnp.concatenate([jnp.full((tq, 1), sink_ref[kvh * n_h + t], jnp.float32) for t in range(n_h)], axis=0)
    m = jnp.maximum(jnp.maximum(jnp.max(s_loc, -1, keepdims=True), jnp.max(s_ctx, -1, keepdims=True)), sk)
    p_loc = jnp.exp(s_loc - m)
    p_ctx = jnp.exp(s_ctx - m)
    den = jnp.sum(p_loc, -1, keepdims=True) + jnp.sum(p_ctx, -1, keepdims=True) + jnp.exp(sk - m)
    o = (_dot(p_loc.astype(_MXU), v) + _dot(p_ctx.astype(_MXU), cv_ref[...].astype(_MXU))) / den
    for t in range(n_h):
        o_ref[:, t * HEAD_DIM:(t + 1) * HEAD_DIM] = o[t * tq:(t + 1) * tq].astype(o_ref.dtype)


def _window_attention(p_arr, o_arr, cache_k, cache_v, jl, sink, *, b_off, n_b, q_col, k_col, v_col):
    G, T, _ = p_arr.shape
    n_h = C_HEADS // C_KV_HEADS
    q_w = n_h * HEAD_DIM
    tq = min(256, T - 2 * WINDOW)
    assert tq & (tq - 1) == 0 and T % tq == 0
    past = cache_k.shape[-2]
    in_specs = [
        pl.BlockSpec((None, tq, q_w), lambda b, h, i: (b + b_off, i, q_col // q_w + h)),
        pl.BlockSpec((None, T, HEAD_DIM), lambda b, h, i: (b + b_off, 0, k_col // HEAD_DIM + h)),
        pl.BlockSpec((None, T, HEAD_DIM), lambda b, h, i: (b + b_off, 0, v_col // HEAD_DIM + h)),
        pl.BlockSpec((None, None, None, past, HEAD_DIM), lambda b, h, i: (b, jl, h, 0, 0)),
        pl.BlockSpec((None, None, None, past, HEAD_DIM), lambda b, h, i: (b, jl, h, 0, 0)),
        pl.BlockSpec(memory_space=pltpu.SMEM),
        pl.BlockSpec((None, tq, q_w), lambda b, h, i: (b + b_off, i, h)),
    ]

    def body(q_ref, k_ref, v_ref, ck_ref, cv_ref, sink_ref, _o_in, o_ref):
        _window_kernel(q_ref, k_ref, v_ref, ck_ref, cv_ref, sink_ref, o_ref, tq=tq, n_h=n_h, seq_len=T)

    return pl.pallas_call(
        body,
        out_shape=jax.ShapeDtypeStruct(o_arr.shape, o_arr.dtype),
        grid=(n_b, C_KV_HEADS, T // tq),
        in_specs=in_specs,
        out_specs=pl.BlockSpec((None, tq, q_w), lambda b, h, i: (b + b_off, i, h)),
        input_output_aliases={6: 0},
        compiler_params=_params("arbitrary", "arbitrary", "arbitrary"),
        name="window_attention",
    )(p_arr, p_arr, p_arr, cache_k, cache_v, sink, o_arr)


def _mlstm_kernel(*refs, nc, has_init, emit_state):
    it = iter(refs)
    qf, kf, vf0, vf1, qb, kb, vb0, vb1 = [next(it) for _ in range(8)]
    gcf, grf, gcb, grb = [next(it) for _ in range(4)]
    if has_init:
        c0_ref, n0_ref, m0_ref = next(it), next(it), next(it)
    _hf_in, _hb_in = next(it), next(it)
    hf_ref, hb_ref = next(it), next(it)
    if emit_state:
        cout_ref, nout_ref, mout_ref = next(it), next(it), next(it)
    c_sc, n_sc, m_sc = next(it), next(it), next(it)
    j = pl.program_id(1)
    L = CHUNK
    nch = 2 * D_HEADS

    @pl.when(j == 0)
    def _():
        if has_init:
            c_sc[...] = c0_ref[...]
            n_sc[...] = n0_ref[...]
            m_sc[...] = m0_ref[...]
        else:
            c_sc[...] = jnp.zeros(c_sc.shape, jnp.float32)
            n_sc[...] = jnp.zeros(n_sc.shape, jnp.float32)
            m_sc[...] = jnp.zeros(m_sc.shape, jnp.float32)

    row = lax.broadcasted_iota(jnp.int32, (L, L), 0)
    col = lax.broadcasted_iota(jnp.int32, (L, L), 1)
    lower = col <= row
    upper = col >= row
    lower_m = jnp.where(lower, 1.0, 0.0).astype(_MXU)
    upper_m = jnp.where(upper, 1.0, 0.0).astype(_MXU)

    def exact_dot(a, b, a_is_data):
        if a_is_data:
            hi, lo = _split_hi_lo(a)
            return _dot(hi, b) + _dot(lo, b)
        hi, lo = _split_hi_lo(b)
        return _dot(a, hi) + _dot(a, lo)

    for d in range(2):
        rev = d == 1
        q_ref, k_ref, v_refs = (qb, kb, (vb0, vb1)) if rev else (qf, kf, (vf0, vf1))
        gc_ref, gr_ref = (gcb, grb) if rev else (gcf, grf)
        h_ref = hb_ref if rev else hf_ref
        gc = gc_ref[...]
        gr = gr_ref[...]
        lf_c = _log_sigmoid(gc)
        lf_r = _log_sigmoid(gr)
        b_cols = exact_dot(upper_m if rev else lower_m, lf_c, False)
        b_rows = exact_dot(lf_r, lower_m if rev else upper_m, True)
        tri = upper if rev else lower
        for hd in range(D_HEADS):
            ci = d * D_HEADS + hd
            ic, fc = d * 2 * D_HEADS + hd, d * 2 * D_HEADS + D_HEADS + hd
            ig_c, ig_r = gc[:, ic:ic + 1], gr[ic:ic + 1, :]
            b_c, b_r = b_cols[:, fc:fc + 1], b_rows[fc:fc + 1, :]
            q = q_ref[:, hd * D_QK:(hd + 1) * D_QK]
            k = k_ref[:, hd * D_QK:(hd + 1) * D_QK]
            v_ref = v_refs[hd // 2]
            v = v_ref[:, (hd % 2) * D_V:(hd % 2 + 1) * D_V]
            m = m_sc[ci:ci + 1, 0:1]
            n = n_sc[ci:ci + 1, :]
            c_mat = c_sc[ci]
            qm, km, vm = q.astype(_MXU), k.astype(_MXU), v.astype(_MXU)

            dmat = jnp.where(tri, b_c - b_r + ig_r, NEG_INF)
            m_prev = b_c + m
            m_t = jnp.maximum(m_prev, jnp.max(dmat, axis=-1, keepdims=True))
            w_intra = jnp.exp(dmat - m_t)
            w_prev = jnp.exp(m_prev - m_t)
            s = _dot_nt(qm, km) * w_intra
            num = w_prev * _dot(qm, c_mat.astype(_MXU)) + _dot(s.astype(_MXU), vm)
            den = w_prev * jnp.sum(q * n, axis=-1, keepdims=True) + jnp.sum(s, axis=-1, keepdims=True)
            h = num / jnp.maximum(jnp.abs(den), jnp.exp(-m_t))
            h_ref[:, hd * D_V:(hd + 1) * D_V] = h.astype(h_ref.dtype)

            b_end = b_c[0:1] if rev else b_c[L - 1:L]
            g_c = b_end - b_c + ig_c
            g_r = b_end - b_r + ig_r
            m_new = jnp.maximum(b_end + m, jnp.max(g_r, axis=-1, keepdims=True))
            wk = jnp.exp(g_c - m_new)
            decay = jnp.exp(b_end + m - m_new)
            c_sc[ci] = decay * c_mat + _dot_tn(km, (wk * v).astype(_MXU))
            n_sc[ci:ci + 1, :] = decay * n + jnp.sum(wk * k, axis=0, keepdims=True)
            m_sc[ci:ci + 1, :] = jnp.broadcast_to(m_new, (1, LANES))

    if emit_state:
        @pl.when(j == nc - 1)
        def _():
            cout_ref[...] = c_sc[...]
            nout_ref[...] = n_sc[...]
            mout_ref[...] = m_sc[...]


def _mlstm(p_arr, gates, gates_t, hf_arr, hb_arr, *, n_seq, seq_len, b_off, q_col, k_col, v_col,
           init=None, emit_state=False):
    G, T, _ = p_arr.shape
    per_b = T // seq_len
    nc = seq_len // CHUNK
    wq = D_HEADS * D_QK
    hw = D_HEADS * D_V
    nch = 2 * D_HEADS

    def rows(s, j, rev):
        jj = nc - 1 - j if rev else j
        return b_off + s // per_b, (s % per_b) * nc + jj

    def col_spec(width, col, rev):
        return pl.BlockSpec((None, CHUNK, width), lambda s, j: rows(s, j, rev) + (col // width,))

    def gate_t_spec(rev):
        def imap(s, j):
            b, r = rows(s, j, rev)
            return b, 0, r
        return pl.BlockSpec((None, N_GATES, CHUNK), imap)

    in_specs, args = [], []
    for rev in (False, True):
        in_specs += [col_spec(wq, q_col, rev), col_spec(wq, k_col, rev),
                     col_spec(wq, v_col, rev), col_spec(wq, v_col + wq, rev)]
        args += [p_arr] * 4
    for rev in (False, True):
        in_specs += [col_spec(LANES, 0, rev), gate_t_spec(rev)]
        args += [gates, gates_t]
    if init is not None:
        c0, n0, m0, jl = init
        in_specs += [pl.BlockSpec((None, None, nch, D_QK, D_V), lambda s, j: (s, jl, 0, 0, 0)),
                     pl.BlockSpec((None, None, nch, D_QK), lambda s, j: (s, jl, 0, 0)),
                     pl.BlockSpec((None, None, nch, LANES), lambda s, j: (s, jl, 0, 0))]
        args += [c0, n0, m0]
    n_in = len(args)
    h_specs = [col_spec(hw, 0, False), col_spec(hw, 0, True)]
    in_specs += h_specs
    args += [hf_arr, hb_arr]
    out_shape = [jax.ShapeDtypeStruct(hf_arr.shape, hf_arr.dtype),
                 jax.ShapeDtypeStruct(hb_arr.shape, hb_arr.dtype)]
    out_specs = list(h_specs)
    if emit_state:
        out_shape += [jax.ShapeDtypeStruct((n_seq, nch, D_QK, D_V), jnp.float32),
                      jax.ShapeDtypeStruct((n_seq, nch, D_QK), jnp.float32),
                      jax.ShapeDtypeStruct((n_seq, nch, LANES), jnp.float32)]
        out_specs += [pl.BlockSpec((None, nch, D_QK, D_V), lambda s, j: (s, 0, 0, 0)),
                      pl.BlockSpec((None, nch, D_QK), lambda s, j: (s, 0, 0)),
                      pl.BlockSpec((None, nch, LANES), lambda s, j: (s, 0, 0))]
    return pl.pallas_call(
        functools.partial(_mlstm_kernel, nc=nc, has_init=init is not None, emit_state=emit_state),
        out_shape=out_shape,
        grid=(n_seq, nc),
        in_specs=in_specs,
        out_specs=out_specs,
        scratch_shapes=[pltpu.VMEM((nch, D_QK, D_V), jnp.float32), pltpu.VMEM((nch, D_QK), jnp.float32),
                        pltpu.VMEM((nch, LANES), jnp.float32)],
        input_output_aliases={n_in: 0, n_in + 1: 1},
        compiler_params=_params("arbitrary", "arbitrary"),
        name="mlstm",
    )(*args)


def _outproj_kernel(*refs, odd, nj, tn, n_experts):
    it = iter(refs)
    if odd:
        oc_ref, hf_ref, hb_ref, do0_ref, do1_ref, dng_ref = [next(it) for _ in range(6)]
    else:
        o_ref = next(it)
    x_ref, g1_ref, w_ref, ng_ref, sh2_ref, sc2_ref, rwh_ref, rwl_ref, rb_ref = [next(it) for _ in range(9)]
    xn_ref, f_ref, ti_ref, tg_ref = [next(it) for _ in range(4)]
    xn_sc = next(it)
    om_sc = next(it) if odd else None
    j = pl.program_id(2)

    if odd:
        @pl.when(j == 0)
        def _():
            wc = oc_ref.shape[1]
            om_sc[:, :wc] = oc_ref[...].astype(_MXU)
            hd = hf_ref[...] + hb_ref[...]
            half = do0_ref.shape[1]
            for hh in range(D_HEADS):
                y = hd[:, hh * D_V:(hh + 1) * D_V]
                y = y * lax.rsqrt(jnp.mean(y * y, axis=-1, keepdims=True) + EPS) * dng_ref[:, hh * D_V:(hh + 1) * D_V]
                do_ref = do0_ref if hh * D_V < half else do1_ref
                c0 = (hh * D_V) % half
                y = y * _sigmoid(do_ref[:, c0:c0 + D_V])
                om_sc[:, wc + hh * D_V:wc + (hh + 1) * D_V] = y.astype(_MXU)
        o = om_sc[...]
    else:
        o = o_ref[...].astype(_MXU)

    xn = x_ref[...] + g1_ref[...] * _dot(o, w_ref[...].astype(_MXU))
    xn_ref[...] = xn
    xn_sc[j] = xn

    @pl.when(j == nj - 1)
    def _():
        ss = jnp.zeros((xn.shape[0], 1), jnp.float32)
        for c in range(nj):
            t = xn_sc[c]
            ss = ss + jnp.sum(t * t, axis=-1, keepdims=True)
        inv = lax.rsqrt(ss / (nj * tn) + EPS)
        logits = rb_ref[...]
        for c in range(nj):
            cs = slice(c * tn, (c + 1) * tn)
            f = (xn_sc[c] * inv * ng_ref[:, cs]) * (1.0 + sc2_ref[:, cs]) + sh2_ref[:, cs]
            f_ref[:, cs] = f.astype(f_ref.dtype)
            fh, fl = _split_hi_lo(f)
            logits = logits + (_dot(fh, rwh_ref[cs, :]) + _dot(fl, rwh_ref[cs, :]) + _dot(fh, rwl_ref[cs, :]))
        lane = lax.broadcasted_iota(jnp.int32, logits.shape, 1)
        cur = logits
        vals, idxs = [], []
        for _ in range(TOP_K):
            mx = jnp.max(cur, axis=-1, keepdims=True)
            ix = jnp.min(jnp.where(cur == mx, lane, LANES), axis=-1, keepdims=True)
            vals.append(mx)
            idxs.append(ix)
            cur = jnp.where(lane == ix, -jnp.inf, cur)
        es = [jnp.exp(v - vals[0]) for v in vals]
        tot = es[0] + es[1] + es[2] + es[3]
        ti = jnp.zeros(logits.shape, jnp.int32)
        tg = jnp.zeros(logits.shape, jnp.float32)
        for kk in range(TOP_K):
            ti = jnp.where(lane == kk, idxs[kk], ti)
            tg = jnp.where(lane == kk, es[kk] / tot, tg)
        ti_ref[...] = ti
        tg_ref[...] = tg


def _out_project(x, mods, layer, w_out, norm_ffn_g, rwh, rwl, rb, n_experts, *, o=None, odd_in=None):
    G, T, D = x.shape
    mix = w_out.shape[1]
    tm = min(512, T)
    tn = _tile(D, 512)
    nj = D // tn
    odd = odd_in is not None
    bmap = lambda b, i, j: (b, i, 0)
    if odd:
        oc, hf, hb, p_cd, do_col, d_norm_g = odd_in
        half = (D_HEADS * D_V) // 2
        in_specs = [pl.BlockSpec((None, tm, oc.shape[-1]), bmap),
                    pl.BlockSpec((None, tm, hf.shape[-1]), bmap),
                    pl.BlockSpec((None, tm, hb.shape[-1]), bmap),
                    pl.BlockSpec((None, tm, half), lambda b, i, j: (b, i, do_col // half)),
                    pl.BlockSpec((None, tm, half), lambda b, i, j: (b, i, do_col // half + 1)),
                    pl.BlockSpec((1, D_HEADS * D_V), lambda b, i, j: (0, 0))]
        args = [oc, hf, hb, p_cd, p_cd, d_norm_g]
    else:
        in_specs = [pl.BlockSpec((None, tm, mix), bmap)]
        args = [o]
    in_specs += [
        pl.BlockSpec((None, tm, tn), lambda b, i, j: (b, i, j)),
        pl.BlockSpec((None, None, None, 1, tn), lambda b, i, j: (layer, 2, b, 0, j)),
        pl.BlockSpec((None, mix, tn), lambda b, i, j: (layer, 0, j)),
        pl.BlockSpec((None, 1, D), lambda b, i, j: (layer, 0, 0)),
        pl.BlockSpec((None, None, None, 1, D), lambda b, i, j: (layer, 3, b, 0, 0)),
        pl.BlockSpec((None, None, None, 1, D), lambda b, i, j: (layer, 4, b, 0, 0)),
        pl.BlockSpec((None, D, LANES), lambda b, i, j: (layer, 0, 0)),
        pl.BlockSpec((None, D, LANES), lambda b, i, j: (layer, 0, 0)),
        pl.BlockSpec((None, 1, LANES), lambda b, i, j: (layer, 0, 0)),
    ]
    args += [x, mods, w_out, norm_ffn_g.reshape(norm_ffn_g.shape[0], 1, D), mods, mods, rwh, rwl, rb]
    out_shape = [jax.ShapeDtypeStruct((G, T, D), jnp.float32),
                 jax.ShapeDtypeStruct((G, T, D), _MXU),
                 jax.ShapeDtypeStruct((G, T, LANES), jnp.int32),
                 jax.ShapeDtypeStruct((G, T, LANES), jnp.float32)]
    out_specs = [pl.BlockSpec((None, tm, tn), lambda b, i, j: (b, i, j)),
                 pl.BlockSpec((None, tm, D), bmap),
                 pl.BlockSpec((None, tm, LANES), bmap),
                 pl.BlockSpec((None, tm, LANES), bmap)]
    scratch = [pltpu.VMEM((nj, tm, tn), jnp.float32)]
    if odd:
        scratch.append(pltpu.VMEM((tm, mix), _MXU))
    return pl.pallas_call(
        functools.partial(_outproj_kernel, odd=odd, nj=nj, tn=tn, n_experts=n_experts),
        out_shape=out_shape,
        grid=(G, T // tm, nj),
        in_specs=in_specs,
        out_specs=out_specs,
        scratch_shapes=scratch,
        compiler_params=_params("arbitrary", "arbitrary", "arbitrary"),
        name="out_project_router",
    )(*args)


def _moe_kernel(te_ref, ns_ref, x_ref, wg_ref, wu_ref, wd_ref, bg_ref, bu_ref, bd_ref, o_ref,
                wg_sc, wu_sc, wd_sc, *, n_sub):
    t = pl.program_id(0)
    c = pl.program_id(1)
    nsub = ns_ref[t]

    @pl.when(nsub > 0)
    def _():
        wg_sc[...] = wg_ref[...].astype(_MXU)
        wu_sc[...] = wu_ref[...].astype(_MXU)
        wd_sc[...] = wd_ref[...].astype(_MXU)

    for s in range(n_sub):
        rs = slice(s * MOE_SUB, (s + 1) * MOE_SUB)

        @pl.when(s < nsub)
        def _(rs=rs):
            xs = x_ref[rs, :]
            gt = _dot(xs, wg_sc[...]) + bg_ref[...]
            up = _dot(xs, wu_sc[...]) + bu_ref[...]
            gt = jnp.minimum(gt, SWIGLU_LIMIT)
            up = jnp.clip(up, -SWIGLU_LIMIT, SWIGLU_LIMIT)
            act = gt * _sigmoid(SWIGLU_ALPHA * gt) * (up + 1.0)
            part = _dot(act.astype(_MXU), wd_sc[...])

            @pl.when(c == 0)
            def _():
                o_ref[rs, :] = part + bd_ref[...]

            @pl.when(c > 0)
            def _():
                o_ref[rs, :] += part

        @pl.when((s >= nsub) & (c == 0))
        def _(rs=rs):
            o_ref[rs, :] = jnp.zeros((MOE_SUB, o_ref.shape[1]), jnp.float32)


def _moe_experts(xb, tile_expert, tile_nsub, w_gate_up, b_gate_up, w_down, b_down, layer):
    P, D = xb.shape
    E, _, F2 = w_gate_up.shape[1:]
    F = F2 // 2
    fc = _tile(F, 512)
    nfc = F // fc
    n_tiles = P // MOE_TILE
    grid_spec = pltpu.PrefetchScalarGridSpec(
        num_scalar_prefetch=2,
        grid=(n_tiles, nfc),
        in_specs=[
            pl.BlockSpec((MOE_TILE, D), lambda t, c, te, ns: (t, 0)),
            pl.BlockSpec((None, None, D, fc), lambda t, c, te, ns: (layer, te[t], 0, c)),
            pl.BlockSpec((None, None, D, fc), lambda t, c, te, ns: (layer, te[t], 0, nfc + c)),
            pl.BlockSpec((None, None, fc, D), lambda t, c, te, ns: (layer, te[t], c, 0)),
            pl.BlockSpec((None, None, 1, fc), lambda t, c, te, ns: (layer, te[t], 0, c)),
            pl.BlockSpec((None, None, 1, fc), lambda t, c, te, ns: (layer, te[t], 0, nfc + c)),
            pl.BlockSpec((None, None, 1, D), lambda t, c, te, ns: (layer, te[t], 0, 0)),
        ],
        out_specs=pl.BlockSpec((MOE_TILE, D), lambda t, c, te, ns: (t, 0)),
        scratch_shapes=[pltpu.VMEM((D, fc), _MXU), pltpu.VMEM((D, fc), _MXU), pltpu.VMEM((fc, D), _MXU)],
    )
    depth = w_gate_up.shape[0]
    return pl.pallas_call(
        functools.partial(_moe_kernel, n_sub=MOE_TILE // MOE_SUB),
        out_shape=jax.ShapeDtypeStruct((P, D), jnp.float32),
        grid_spec=grid_spec,
        compiler_params=_params("arbitrary", "arbitrary"),
        name="moe_experts",
    )(tile_expert, tile_nsub, xb, w_gate_up, w_gate_up, w_down,
      b_gate_up.reshape(depth, E, 1, F2), b_gate_up.reshape(depth, E, 1, F2), b_down.reshape(depth, E, 1, D))


def _moe_layer(f_tok, top_idx, top_gate, w_gate_up, b_gate_up, w_down, b_down, layer):
    N, D = f_tok.shape
    E = w_gate_up.shape[1]
    NK = N * TOP_K
    e_flat = top_idx.reshape(-1)
    order = jnp.argsort(e_flat, stable=True).astype(jnp.int32)
    e_s = e_flat[order]
    tok_s = order // TOP_K
    counts = jnp.zeros((E,), jnp.int32).at[e_flat].add(1)
    starts = jnp.cumsum(counts) - counts
    padded = (counts + MOE_TILE - 1) // MOE_TILE * MOE_TILE
    pad_end = jnp.cumsum(padded)
    pad_start = pad_end - padded
    dest = pad_start[e_s] + (jnp.arange(NK, dtype=jnp.int32) - starts[e_s])
    n_tiles = (NK + MOE_TILE - 1) // MOE_TILE + E
    P = n_tiles * MOE_TILE
    buf_tok = jnp.full((P,), N, jnp.int32).at[dest].set(tok_s)
    pos = jnp.zeros((NK,), jnp.int32).at[order].set(dest).reshape(N, TOP_K)
    tile_row0 = jnp.arange(n_tiles, dtype=jnp.int32) * MOE_TILE
    tile_expert = jnp.minimum(jnp.searchsorted(pad_end, tile_row0, side="right"), E - 1).astype(jnp.int32)
    valid = jnp.clip(counts[tile_expert] - (tile_row0 - pad_start[tile_expert]), 0, MOE_TILE)
    valid = jnp.where(tile_row0 < pad_end[-1], valid, 0)
    tile_nsub = ((valid + MOE_SUB - 1) // MOE_SUB).astype(jnp.int32)
    x_pad = jnp.concatenate([f_tok, jnp.zeros((1, D), f_tok.dtype)], axis=0)
    xb = x_pad[buf_tok]
    yb = _moe_experts(xb, tile_expert, tile_nsub, w_gate_up, b_gate_up, w_down, b_down, layer)
    return jnp.sum(yb[pos] * top_gate[:, :, None], axis=1)


def _residual_kernel(x_ref, y_ref, g_ref, o_ref):
    o_ref[...] = x_ref[...] + g_ref[...] * y_ref[...]


def _final_kernel(x_ref, y_ref, g_ref, ng_ref, o_ref):
    x = x_ref[...] + g_ref[...] * y_ref[...]
    o_ref[...] = x * lax.rsqrt(jnp.mean(x * x, axis=-1, keepdims=True) + EPS) * ng_ref[...]


def _residual(x, y, mods, layer, final_g=None):
    G, T, D = x.shape
    tm = min(512, T)
    bmap = lambda b, i: (b, i, 0)
    in_specs = [pl.BlockSpec((None, tm, D), bmap), pl.BlockSpec((None, tm, D), bmap),
                pl.BlockSpec((None, None, None, 1, D), lambda b, i: (layer, 5, b, 0, 0))]
    args = [x, y, mods]
    body = _residual_kernel
    if final_g is not None:
        in_specs.append(pl.BlockSpec((1, D), lambda b, i: (0, 0)))
        args.append(final_g.reshape(1, D))
        body = _final_kernel
    return pl.pallas_call(
        body,
        out_shape=jax.ShapeDtypeStruct((G, T, D), jnp.float32),
        grid=(G, T // tm),
        in_specs=in_specs,
        out_specs=pl.BlockSpec((None, tm, D), bmap),
        compiler_params=_params("arbitrary", "arbitrary"),
        name="gated_residual",
    )(*args)


def _rope_tables(T):
    n_rows = T // GRID_W
    row = jnp.repeat(jnp.arange(n_rows, dtype=jnp.float32), GRID_W)
    col = jnp.tile(jnp.arange(GRID_W, dtype=jnp.float32), n_rows)
    inv = ROPE_THETA ** (-jnp.arange(0, AXIS_DIM, 2, dtype=jnp.float32) / AXIS_DIM)
    cr, sr = jnp.cos(row[:, None] * inv), jnp.sin(row[:, None] * inv)
    cc, sc = jnp.cos(col[:, None] * inv), jnp.sin(col[:, None] * inv)
    cos_t = jnp.concatenate([cr, cr, cc, cc], axis=-1)
    sin_t = jnp.concatenate([-sr, sr, -sc, sc], axis=-1)
    return (jnp.stack([jnp.ones_like(cos_t), cos_t]), jnp.stack([jnp.zeros_like(sin_t), sin_t]))


def _heads_out(p0, col, n, d, Bp, Tp):
    return p0[:, col:col + n * d].reshape(Bp, Tp, n, d).transpose(0, 2, 1, 3)


def kernel(x_prompt, x_sample, cache_a_k, cache_a_v, cache_b_k, cache_b_v, cache_c_k, cache_c_v, state_d_C, state_d_n, state_d_m, c, c_ctx, norm_mix_g, norm_ffn_g, w_mod, b_mod, w_out, w_in_ab, a_q_norm_g, a_k_norm_g, b_lambda, b_subln_g, w_in_cd, b_gates, c_sink, d_norm_g, router_w, router_b, w_gate_up, b_gate_up, w_down, b_down, final_norm_g):
    Bp, Tp, D = x_prompt.shape
    Bs, T, _ = x_sample.shape
    assert Bp * Tp == T, "context tokens are laid out as one extra batch of DEC_SEQ tokens"
    depth = w_mod.shape[0]
    E = router_w.shape[-1]
    past = cache_a_k.shape[-2]
    G = 1 + Bs
    scale = HEAD_DIM ** -0.5
    f32 = jnp.float32

    x = jnp.concatenate([x_prompt.reshape(1, T, D), x_sample], axis=0)
    cond = jnp.concatenate([c_ctx[None], c, jnp.zeros((8 - G, D), f32)], axis=0)
    mods = _modulation(cond, w_mod, b_mod)
    mods = mods[:, :G].reshape(depth, G, 6, 1, D).transpose(0, 2, 1, 3, 4)

    cos_t, sin_t = _rope_tables(T)
    pad_e = LANES - E
    rw = jnp.pad(router_w, ((0, 0), (0, 0), (0, pad_e)))
    rwh = rw.astype(_MXU)
    rwl = (rw - rwh.astype(f32)).astype(_MXU)
    rb = jnp.pad(router_b, ((0, 0), (0, pad_e)), constant_values=NEG_INF).reshape(depth, 1, LANES)

    ab_ops = ([("q", True, scale)] * A_HEADS + [("k", True, 1.0)] * A_KV_HEADS + [(None, False, 1.0)] * A_KV_HEADS
              + [(None, True, scale)] * (2 * B_HEADS) + [(None, True, 1.0)] * (2 * B_HEADS)
              + [(None, False, 1.0)] * (B_HEADS * B_VDIM // HEAD_DIM))
    cd_ops = ([(None, True, scale)] * C_HEADS + [(None, True, 1.0)] * C_KV_HEADS + [(None, False, 1.0)] * C_KV_HEADS
              + [(None, False, 1.0)] * D_HEADS + [(None, False, D_QK ** -0.5)] * D_HEADS
              + [(None, False, 1.0)] * (2 * D_HEADS * D_V // HEAD_DIM))
    A_Q, A_K, A_V = 0, A_HEADS * HEAD_DIM, (A_HEADS + A_KV_HEADS) * HEAD_DIM
    B_Q = (A_HEADS + 2 * A_KV_HEADS) * HEAD_DIM
    B_K = B_Q + 2 * B_HEADS * HEAD_DIM
    B_V = B_K + 2 * B_HEADS * HEAD_DIM
    C_Q, C_K, C_V = 0, C_HEADS * HEAD_DIM, (C_HEADS + C_KV_HEADS) * HEAD_DIM
    D_Q = (C_HEADS + 2 * C_KV_HEADS) * HEAD_DIM
    D_K = D_Q + D_HEADS * D_QK
    D_VC = D_K + D_HEADS * D_QK
    D_O = D_VC + D_HEADS * D_V
    a_grp = A_HEADS // A_KV_HEADS
    c_grp = C_HEADS // C_KV_HEADS
    ones_g = jnp.ones((1, HEAD_DIM), f32)

    new = {k: [] for k in ("ak", "av", "bk", "bv", "ck", "cv", "dC", "dn", "dm")}

    for layer in range(depth):
        jl = layer // 2
        if layer % 2 == 0:
            lam_init = 0.8 - 0.6 * math.exp(-0.3 * layer)
            p = _project(x, mods, layer, norm_mix_g, w_in_ab, jl, AB_WIDTH, ab_ops, cos_t, sin_t,
                         a_q_norm_g[jl].reshape(1, HEAD_DIM), a_k_norm_g[jl].reshape(1, HEAD_DIM))
            p0 = p[0]
            new["ak"].append(_heads_out(p0, A_K, A_KV_HEADS, HEAD_DIM, Bp, Tp))
            new["av"].append(_heads_out(p0, A_V, A_KV_HEADS, HEAD_DIM, Bp, Tp))
            new["bk"].append(p0[:, B_K:B_V].reshape(Bp, Tp, B_HEADS, 2, HEAD_DIM).transpose(0, 3, 2, 1, 4))
            new["bv"].append(_heads_out(p0, B_V, B_HEADS, B_VDIM, Bp, Tp))

            o = jnp.zeros((G, T, MIX_WIDTH), _MXU)
            a_groups = ((tuple(t * HEAD_DIM for t in range(a_grp)), 0),)
            b_groups = (((0,), 0), ((HEAD_DIM,), HEAD_DIM))
            diff = (b_lambda[jl], b_subln_g[jl].reshape(1, B_VDIM), lam_init)
            a_common = dict(n_kvh=A_KV_HEADS, groups=a_groups, q_col=A_Q, q_w=a_grp * HEAD_DIM, k_col=A_K,
                            k_w=HEAD_DIM, v_col=A_V, dv=HEAD_DIM, o_col=0, o_w=a_grp * HEAD_DIM)
            b_common = dict(n_kvh=B_HEADS, groups=b_groups, q_col=B_Q, q_w=2 * HEAD_DIM, k_col=B_K,
                            k_w=2 * HEAD_DIM, v_col=B_V, dv=B_VDIM, o_col=A_HEADS * HEAD_DIM, o_w=B_VDIM, diff=diff)
            o = _attention(p, o, n_seq=Bp, seq_len=Tp, b_off=0, tq=min(256, Tp), name="ctx_attn_a", **a_common)
            o = _attention(p, o, n_seq=Bp, seq_len=Tp, b_off=0, tq=min(256, Tp), name="ctx_attn_b", **b_common)
            cache_a = ([(cache_a_k, pl.BlockSpec((None, None, None, past, HEAD_DIM),
                                                  lambda s, h, i: (s, jl, h, 0, 0)))],
                       (cache_a_v, pl.BlockSpec((None, None, None, past, HEAD_DIM),
                                                lambda s, h, i: (s, jl, h, 0, 0))))
            cache_b = ([(cache_b_k, pl.BlockSpec((None, None, None, None, past, HEAD_DIM),
                                                  lambda s, h, i, mm=mm: (s, jl, mm, h, 0, 0))) for mm in range(2)],
                       (cache_b_v, pl.BlockSpec((None, None, None, past, B_VDIM),
                                                lambda s, h, i: (s, jl, h, 0, 0))))
            o = _attention(p, o, n_seq=Bs, seq_len=T, b_off=1, tq=min(256, T), cache=cache_a,
                           name="latent_attn_a", **a_common)
            o = _attention(p, o, n_seq=Bs, seq_len=T, b_off=1, tq=min(512, T), cache=cache_b,
                           name="latent_attn_b", **b_common)
            x, f_tok, top_i, top_g = _out_project(x, mods, layer, w_out, norm_ffn_g, rwh, rwl, rb, E, o=o)
        else:
            wg = jnp.pad(w_in_cd[jl][:, CD_MAIN:], ((0, 0), (0, LANES - N_GATES)))
            bg = jnp.pad(b_gates[jl], (0, LANES - N_GATES)).reshape(1, LANES)
            p, gates = _project(x, mods, layer, norm_mix_g, w_in_cd, jl, CD_MAIN, cd_ops, cos_t, sin_t,
                                ones_g, ones_g, gates=(wg, bg))
            gates_t = jnp.swapaxes(gates[:, :, :N_GATES], 1, 2)
            p0 = p[0]
            new["ck"].append(_heads_out(p0, C_K, C_KV_HEADS, HEAD_DIM, Bp, Tp))
            new["cv"].append(_heads_out(p0, C_V, C_KV_HEADS, HEAD_DIM, Bp, Tp))

            oc = jnp.zeros((G, T, C_HEADS * HEAD_DIM), _MXU)
            c_groups = ((tuple(t * HEAD_DIM for t in range(c_grp)), 0),)
            oc = _attention(p, oc, n_seq=Bp, seq_len=Tp, b_off=0, tq=min(256, Tp), n_kvh=C_KV_HEADS,
                            groups=c_groups, q_col=C_Q, q_w=c_grp * HEAD_DIM, k_col=C_K, k_w=HEAD_DIM,
                            v_col=C_V, dv=HEAD_DIM, o_col=0, o_w=c_grp * HEAD_DIM, sink=c_sink[jl],
                            name="ctx_attn_c")
            oc = _window_attention(p, oc, cache_c_k, cache_c_v, jl, c_sink[jl], b_off=1, n_b=Bs,
                                   q_col=C_Q, k_col=C_K, v_col=C_V)
            hf = jnp.zeros((G, T, D_HEADS * D_V), f32)
            hb = jnp.zeros((G, T, D_HEADS * D_V), f32)
            hf, hb, dC, dn, dm = _mlstm(p, gates, gates_t, hf, hb, n_seq=Bp, seq_len=Tp, b_off=0,
                                        q_col=D_Q, k_col=D_K, v_col=D_VC, emit_state=True)
            new["dC"].append(dC.reshape(Bp, 2, D_HEADS, D_QK, D_V))
            new["dn"].append(dn.reshape(Bp, 2, D_HEADS, D_QK))
            new["dm"].append(dm[:, :, 0].reshape(Bp, 2, D_HEADS))
            n_odd = state_d_C.shape[1]
            init = (state_d_C.reshape(Bs, n_odd, 2 * D_HEADS, D_QK, D_V),
                    state_d_n.reshape(Bs, n_odd, 2 * D_HEADS, D_QK),
                    jnp.broadcast_to(state_d_m.reshape(Bs, n_odd, 2 * D_HEADS, 1), (Bs, n_odd, 2 * D_HEADS, LANES)),
                    jl)
            hf, hb = _mlstm(p, gates, gates_t, hf, hb, n_seq=Bs, seq_len=T, b_off=1,
                            q_col=D_Q, k_col=D_K, v_col=D_VC, init=init)
            x, f_tok, top_i, top_g = _out_project(
                x, mods, layer, w_out, norm_ffn_g, rwh, rwl, rb, E,
                odd_in=(oc, hf, hb, p, D_O, d_norm_g[jl].reshape(1, D_HEADS * D_V)))

        y = _moe_layer(f_tok.reshape(G * T, D), top_i.reshape(G * T, LANES)[:, :TOP_K],
                       top_g.reshape(G * T, LANES)[:, :TOP_K], w_gate_up, b_gate_up, w_down, b_down, layer)
        x = _residual(x, y.reshape(G, T, D), mods, layer, final_norm_g if layer == depth - 1 else None)

    y_prompt = x[0].reshape(Bp, Tp, D)
    y_sample = x[1:]
    return (y_prompt, y_sample,
            jnp.stack(new["ak"], 1), jnp.stack(new["av"], 1), jnp.stack(new["bk"], 1), jnp.stack(new["bv"], 1),
            jnp.stack(new["ck"], 1), jnp.stack(new["cv"], 1), jnp.stack(new["dC"], 1), jnp.stack(new["dn"], 1),
            jnp.stack(new["dm"], 1))
```

```python
import functools
import math

import jax
import jax.numpy as jnp
from jax import lax
from jax.experimental import pallas as pl
from jax.experimental.pallas import tpu as pltpu

HEAD_DIM = 128
GRID_W = 64
AXIS_DIM = HEAD_DIM // 2
ROPE_THETA = 10000.0
A_HEADS = 8
A_KV_HEADS = 2
B_HEADS = 4
B_VDIM = 2 * HEAD_DIM
C_HEADS = 8
C_KV_HEADS = 2
WINDOW = 128
D_HEADS = 4
D_QK = HEAD_DIM
D_V = 2 * HEAD_DIM
CHUNK = 128
MIX_WIDTH = A_HEADS * HEAD_DIM + B_HEADS * B_VDIM
AB_WIDTH = A_HEADS * HEAD_DIM + 2 * A_KV_HEADS * HEAD_DIM + 2 * B_HEADS * 2 * HEAD_DIM + B_HEADS * B_VDIM
CD_MAIN = C_HEADS * HEAD_DIM + 2 * C_KV_HEADS * HEAD_DIM + 2 * D_HEADS * D_QK + 2 * D_HEADS * D_V
N_GATES = 4 * D_HEADS
TOP_K = 4
SWIGLU_LIMIT = 7.0
SWIGLU_ALPHA = 1.702
EPS = 1e-6
NEG_INF = -1e30
LANES = 128
MOE_TILE = 1024
MOE_SUB = 256
VMEM_LIMIT = 56 * 1024 * 1024

_MXU = jnp.bfloat16


def _dot(a, b):
    return jnp.dot(a, b, preferred_element_type=jnp.float32)


def _dot_nt(a, b):
    return lax.dot_general(a, b, (((1,), (1,)), ((), ())), preferred_element_type=jnp.float32)


def _dot_tn(a, b):
    return lax.dot_general(a, b, (((0,), (0,)), ((), ())), preferred_element_type=jnp.float32)


def _split_hi_lo(a):
    hi = a.astype(_MXU)
    lo = (a - hi.astype(jnp.float32)).astype(_MXU)
    return hi, lo


def _pack_bf16_pair(lo, hi):
    lo_b = lax.bitcast_convert_type(lo.astype(jnp.bfloat16).astype(jnp.float32), jnp.uint32)
    hi_b = lax.bitcast_convert_type(hi.astype(jnp.bfloat16).astype(jnp.float32), jnp.uint32)
    return (hi_b & jnp.uint32(0xFFFF0000)) | (lo_b >> 16)


def _unpack_bf16_pair(w):
    lo = lax.bitcast_convert_type(w << 16, jnp.float32)
    hi = lax.bitcast_convert_type(w & jnp.uint32(0xFFFF0000), jnp.float32)
    return lo.astype(_MXU), hi.astype(_MXU)


def _sigmoid(x):
    return 1.0 / (1.0 + jnp.exp(-x))


def _log_sigmoid(x):
    return -(jnp.maximum(-x, 0.0) + jnp.log1p(jnp.exp(-jnp.abs(x))))


def _tile(n, cap):
    if n <= cap:
        return n
    t = cap - cap % LANES
    while n % t:
        t -= LANES
    return t


def _params(*sem):
    return pltpu.CompilerParams(dimension_semantics=sem, vmem_limit_bytes=VMEM_LIMIT)


def _mod_kernel(s_ref, w_ref, b_ref, o_ref):
    s = s_ref[...]
    s = s * _sigmoid(s)
    o_ref[...] = _dot(s.astype(_MXU), w_ref[...].astype(_MXU)) + b_ref[...]


def _modulation(cond_rows, w_mod, b_mod):
    depth, d, n6 = w_mod.shape
    rows = cond_rows.shape[0]
    tn = _tile(n6, 1024)
    return pl.pallas_call(
        _mod_kernel,
        out_shape=jax.ShapeDtypeStruct((depth, rows, n6), jnp.float32),
        grid=(depth, n6 // tn),
        in_specs=[
            pl.BlockSpec((rows, d), lambda l, j: (0, 0)),
            pl.BlockSpec((None, d, tn), lambda l, j: (l, 0, j)),
            pl.BlockSpec((None, 1, tn), lambda l, j: (l, 0, j)),
        ],
        out_specs=pl.BlockSpec((None, rows, tn), lambda l, j: (l, 0, j)),
        compiler_params=_params("arbitrary", "arbitrary"),
        name="modulation",
    )(cond_rows, w_mod, b_mod.reshape(depth, 1, n6))


def _rope(y, cos, sin_signed):
    lane = lax.broadcasted_iota(jnp.int32, y.shape, 1)
    first = (lane % (2 * (AXIS_DIM // 2))) < (AXIS_DIM // 2)
    partner = jnp.where(first, pltpu.roll(y, HEAD_DIM - AXIS_DIM // 2, 1), pltpu.roll(y, AXIS_DIM // 2, 1))
    return y * cos + partner * sin_signed


def _head_rms(y, g):
    return y * lax.rsqrt(jnp.mean(y * y, axis=-1, keepdims=True) + EPS) * g


def _proj_kernel(x_ref, g_ref, sh_ref, sc_ref, w_ref, cos_ref, sin_ref, qg_ref, kg_ref, *rest,
                 runs, tn, with_gates):
    if with_gates:
        wg_ref, bg_ref, o_ref, og_ref, h_sc = rest
    else:
        o_ref, h_sc = rest
    j = pl.program_id(2)

    @pl.when(j == 0)
    def _():
        x = x_ref[...]
        y = x * lax.rsqrt(jnp.mean(x * x, axis=-1, keepdims=True) + EPS) * g_ref[...]
        h = y * (1.0 + sc_ref[...]) + sh_ref[...]
        h_sc[...] = h.astype(_MXU)
        if with_gates:
            og_ref[...] = _dot(h_sc[...], wg_ref[...].astype(_MXU)) + bg_ref[...]

    acc = _dot(h_sc[...], w_ref[...].astype(_MXU))

    for lo, hi, ops in runs:
        @pl.when((j >= lo) & (j < hi))
        def _(ops=ops):
            for t, (norm, rope, scale) in enumerate(ops):
                y = acc[:, t * HEAD_DIM:(t + 1) * HEAD_DIM]
                if norm == "q":
                    y = _head_rms(y, qg_ref[...])
                elif norm == "k":
                    y = _head_rms(y, kg_ref[...])
                if rope:
                    y = _rope(y, cos_ref[...], sin_ref[...])
                if scale != 1.0:
                    y = y * scale
                o_ref[:, t * HEAD_DIM:(t + 1) * HEAD_DIM] = y


def _tile_runs(head_ops, tn):
    per = tn // HEAD_DIM
    tiles = [tuple(head_ops[i * per:(i + 1) * per]) for i in range(len(head_ops) // per)]
    runs, lo = [], 0
    for i in range(1, len(tiles) + 1):
        if i == len(tiles) or tiles[i] != tiles[lo]:
            runs.append((lo, i, tiles[lo]))
            lo = i
    return tuple(runs)


def _project(x, mods, layer, norm_g, w_in, jl, n_main, head_ops, cos_t, sin_t, qg, kg, gates=None):
    G, T, D = x.shape
    tm = min(1024, T)
    tn = _tile(n_main, 512)
    runs = _tile_runs(head_ops, tn)
    with_gates = gates is not None
    in_specs = [
        pl.BlockSpec((None, tm, D), lambda b, i, j: (b, i, 0)),
        pl.BlockSpec((None, 1, D), lambda b, i, j: (layer, 0, 0)),
        pl.BlockSpec((None, None, None, 1, D), lambda b, i, j: (layer, 0, b, 0, 0)),
        pl.BlockSpec((None, None, None, 1, D), lambda b, i, j: (layer, 1, b, 0, 0)),
        pl.BlockSpec((None, D, tn), lambda b, i, j: (jl, 0, j)),
        pl.BlockSpec((None, tm, HEAD_DIM), lambda b, i, j: (jnp.minimum(b, 1), i, 0)),
        pl.BlockSpec((None, tm, HEAD_DIM), lambda b, i, j: (jnp.minimum(b, 1), i, 0)),
        pl.BlockSpec((1, HEAD_DIM), lambda b, i, j: (0, 0)),
        pl.BlockSpec((1, HEAD_DIM), lambda b, i, j: (0, 0)),
    ]
    args = [x, norm_g.reshape(norm_g.shape[0], 1, D), mods, mods, w_in, cos_t, sin_t, qg, kg]
    out_shape = [jax.ShapeDtypeStruct((G, T, n_main), jnp.float32)]
    out_specs = [pl.BlockSpec((None, tm, tn), lambda b, i, j: (b, i, j))]
    if with_gates:
        wg, bg = gates
        in_specs += [pl.BlockSpec((D, LANES), lambda b, i, j: (0, 0)),
                     pl.BlockSpec((1, LANES), lambda b, i, j: (0, 0))]
        args += [wg, bg]
        out_shape.append(jax.ShapeDtypeStruct((G, T, LANES), jnp.float32))
        out_specs.append(pl.BlockSpec((None, tm, LANES), lambda b, i, j: (b, i, 0)))
    res = pl.pallas_call(
        functools.partial(_proj_kernel, runs=runs, tn=tn, with_gates=with_gates),
        out_shape=out_shape,
        grid=(G, T // tm, n_main // tn),
        in_specs=in_specs,
        out_specs=out_specs,
        scratch_shapes=[pltpu.VMEM((tm, D), _MXU)],
        compiler_params=_params("arbitrary", "arbitrary", "arbitrary"),
        name="norm_mod_project",
    )(*args)
    return res if with_gates else res[0]


def _flash_kernel(*refs, groups, tq, tk, n_lat, dv, has_cache, has_sink, diff, lam_init):
    it = iter(refs)
    q_ref, k_ref, v_ref = next(it), next(it), next(it)
    ck_refs = [next(it) for _ in groups] if has_cache else []
    cv_ref = next(it) if has_cache else None
    sink_ref = next(it) if has_sink else None
    if diff:
        lam_ref, subg_ref = next(it), next(it)
    o_ref = next(it)
    scr = [(next(it), next(it), next(it)) for _ in groups]
    kvh = pl.program_id(1)

    qs = []
    for gi, (q_offs, k_off) in enumerate(groups):
        q = jnp.concatenate([q_ref[:, o:o + HEAD_DIM] for o in q_offs], axis=0).astype(_MXU)
        qs.append(q)
        m_sc, l_sc, acc_sc = scr[gi]
        if has_sink:
            n_h = len(q_offs)
            m_sc[...] = jnp.concatenate(
                [jnp.full((tq, LANES), sink_ref[kvh * n_h + t], jnp.float32) for t in range(n_h)], axis=0)
            lane = lax.broadcasted_iota(jnp.int32, l_sc.shape, 1)
            l_sc[...] = jnp.where(lane == 0, 1.0, 0.0)
        else:
            m_sc[...] = jnp.full(m_sc.shape, NEG_INF, jnp.float32)
            l_sc[...] = jnp.zeros(l_sc.shape, jnp.float32)
        acc_sc[...] = jnp.zeros(acc_sc.shape, jnp.float32)

    def update(gi, k, v):
        m_sc, l_sc, acc_sc = scr[gi]
        s = _dot_nt(qs[gi], k)
        tiles = [s[:, c * LANES:(c + 1) * LANES] for c in range(s.shape[1] // LANES)]
        mx = functools.reduce(jnp.maximum, tiles)
        m_prev = m_sc[...]
        m_new = jnp.maximum(m_prev, jnp.max(mx, axis=-1, keepdims=True))
        alpha = jnp.exp(m_prev - m_new)
        ps = [jnp.exp(t - m_new) for t in tiles]
        l_sc[...] = alpha * l_sc[...] + functools.reduce(jnp.add, ps)
        pv = _dot(jnp.concatenate(ps, axis=-1).astype(_MXU), v)
        for c in range(dv // LANES):
            cs = slice(c * LANES, (c + 1) * LANES)
            acc_sc[:, cs] = alpha * acc_sc[:, cs] + pv[:, cs]
        m_sc[...] = m_new

    def chunk(c, carry):
        r0 = pl.multiple_of(c * tk, tk)
        v = v_ref[pl.ds(r0, tk), :].astype(_MXU)
        for gi, (q_offs, k_off) in enumerate(groups):
            k = k_ref[pl.ds(r0, tk), k_off:k_off + HEAD_DIM].astype(_MXU)
            update(gi, k, v)
        return carry

    lax.fori_loop(0, n_lat // tk, chunk, 0)
    if has_cache:
        v = cv_ref[...].astype(_MXU)
        for gi in range(len(groups)):
            update(gi, ck_refs[gi][...].astype(_MXU), v)

    outs = []
    for gi in range(len(groups)):
        m_sc, l_sc, acc_sc = scr[gi]
        outs.append(acc_sc[...] / jnp.sum(l_sc[...], axis=-1, keepdims=True))
    if diff:
        l32 = lam_ref[...]
        lam = (jnp.exp(jnp.sum(l32[0:1] * l32[1:2], axis=-1, keepdims=True))
               - jnp.exp(jnp.sum(l32[2:3] * l32[3:4], axis=-1, keepdims=True)) + lam_init)
        ob = outs[0] - lam * outs[1]
        ob = ob * lax.rsqrt(jnp.mean(ob * ob, axis=-1, keepdims=True) + EPS) * subg_ref[...]
        o_ref[...] = (ob * (1.0 - lam_init)).astype(o_ref.dtype)
    else:
        o = outs[0]
        for t in range(len(groups[0][0])):
            o_ref[:, t * dv:(t + 1) * dv] = o[t * tq:(t + 1) * tq].astype(o_ref.dtype)


def _attention(p_arr, o_arr, *, n_seq, seq_len, b_off, n_kvh, groups, q_col, q_w, k_col, k_w, v_col, dv,
               o_col, o_w, tq, cache=None, sink=None, diff=None, name):
    G, T, _ = p_arr.shape
    per_b = T // seq_len
    nq = seq_len // tq
    tk = min(512, seq_len)

    def row_map(s, i, blocks_per_seq):
        return b_off + s // per_b, (s % per_b) * blocks_per_seq + i

    def q_map(s, h, i):
        b, r = row_map(s, i, nq)
        return b, r, q_col // q_w + h

    def k_map(s, h, i):
        b, r = row_map(s, 0, 1)
        return b, r, k_col // k_w + h

    def v_map(s, h, i):
        b, r = row_map(s, 0, 1)
        return b, r, v_col // dv + h

    def o_map(s, h, i):
        b, r = row_map(s, i, nq)
        return b, r, o_col // o_w + h

    in_specs = [pl.BlockSpec((None, tq, q_w), q_map),
                pl.BlockSpec((None, seq_len, k_w), k_map),
                pl.BlockSpec((None, seq_len, dv), v_map)]
    args = [p_arr, p_arr, p_arr]
    if cache is not None:
        ck_list, cv = cache
        for arr, spec in ck_list:
            in_specs.append(spec)
            args.append(arr)
        in_specs.append(cv[1])
        args.append(cv[0])
    if sink is not None:
        in_specs.append(pl.BlockSpec(memory_space=pltpu.SMEM))
        args.append(sink)
    lam_init = 0.0
    if diff is not None:
        lam_arr, subg, lam_init = diff
        in_specs += [pl.BlockSpec((4, HEAD_DIM), lambda s, h, i: (0, 0)),
                     pl.BlockSpec((1, dv), lambda s, h, i: (0, 0))]
        args += [lam_arr, subg]
    n_in = len(args)
    in_specs.append(pl.BlockSpec((None, tq, o_w), o_map))
    args.append(o_arr)
    scratch = []
    for q_offs, _ in groups:
        m = len(q_offs) * tq
        scratch += [pltpu.VMEM((m, LANES), jnp.float32), pltpu.VMEM((m, LANES), jnp.float32),
                    pltpu.VMEM((m, dv), jnp.float32)]

    def body(*refs):
        refs = refs[:n_in] + refs[n_in + 1:]
        _flash_kernel(*refs, groups=groups, tq=tq, tk=tk, n_lat=seq_len, dv=dv,
                      has_cache=cache is not None, has_sink=sink is not None,
                      diff=diff is not None, lam_init=lam_init)

    return pl.pallas_call(
        body,
        out_shape=jax.ShapeDtypeStruct(o_arr.shape, o_arr.dtype),
        grid=(n_seq, n_kvh, nq),
        in_specs=in_specs,
        out_specs=pl.BlockSpec((None, tq, o_w), o_map),
        scratch_shapes=scratch,
        input_output_aliases={n_in: 0},
        compiler_params=_params("arbitrary", "arbitrary", "arbitrary"),
        name=name,
    )(*args)


def _window_kernel(q_ref, k_ref, v_ref, ck_ref, cv_ref, sink_ref, o_ref, *, tq, n_h, seq_len):
    kvh = pl.program_id(1)
    i = pl.program_id(2)
    span = tq + 2 * WINDOW
    start = jnp.clip(i * tq - WINDOW, 0, seq_len - span)
    start = pl.multiple_of(start, WINDOW)
    k = k_ref[pl.ds(start, span), :].astype(_MXU)
    v = v_ref[pl.ds(start, span), :].astype(_MXU)
    q = jnp.concatenate([q_ref[:, t * HEAD_DIM:(t + 1) * HEAD_DIM] for t in range(n_h)], axis=0).astype(_MXU)
    m_rows = n_h * tq
    s_loc = _dot_nt(q, k)
    row = lax.broadcasted_iota(jnp.int32, (m_rows, span), 0)
    col = lax.broadcasted_iota(jnp.int32, (m_rows, span), 1)
    qpos = i * tq + (row & (tq - 1))
    kpos = start + col
    s_loc = jnp.where(jnp.abs(qpos - kpos) <= WINDOW, s_loc, NEG_INF)
    s_ctx = _dot_nt(q, ck_ref[...].astype(_MXU))
    sk = jnp.concatenate([jnp.full((tq, LANES), sink_ref[kvh * n_h + t], jnp.float32) for t in range(n_h)], axis=0)
    loc_tiles = [s_loc[:, c * LANES:(c + 1) * LANES] for c in range(span // LANES)]
    ctx_tiles = [s_ctx[:, c * LANES:(c + 1) * LANES] for c in range(s_ctx.shape[1] // LANES)]
    mx = functools.reduce(jnp.maximum, loc_tiles + ctx_tiles)
    m = jnp.maximum(jnp.max(mx, axis=-1, keepdims=True), sk)
    p_loc = [jnp.exp(t - m) for t in loc_tiles]
    p_ctx = [jnp.exp(t - m) for t in ctx_tiles]
    den = (jnp.sum(functools.reduce(jnp.add, p_loc + p_ctx), axis=-1, keepdims=True)
           + jnp.exp(sk - m)[:, 0:1])
    o = (_dot(jnp.concatenate(p_loc, axis=-1).astype(_MXU), v)
         + _dot(jnp.concatenate(p_ctx, axis=-1).astype(_MXU), cv_ref[...].astype(_MXU))) / den
    for t in range(n_h):
        o_ref[:, t * HEAD_DIM:(t + 1) * HEAD_DIM] = o[t * tq:(t + 1) * tq].astype(o_ref.dtype)


def _window_attention(p_arr, o_arr, cache_k, cache_v, jl, sink, *, b_off, n_b, q_col, k_col, v_col):
    G, T, _ = p_arr.shape
    n_h = C_HEADS // C_KV_HEADS
    q_w = n_h * HEAD_DIM
    tq = min(256, T - 2 * WINDOW)
    assert tq & (tq - 1) == 0 and T % tq == 0
    past = cache_k.shape[-2]
    in_specs = [
        pl.BlockSpec((None, tq, q_w), lambda b, h, i: (b + b_off, i, q_col // q_w + h)),
        pl.BlockSpec((None, T, HEAD_DIM), lambda b, h, i: (b + b_off, 0, k_col // HEAD_DIM + h)),
        pl.BlockSpec((None, T, HEAD_DIM), lambda b, h, i: (b + b_off, 0, v_col // HEAD_DIM + h)),
        pl.BlockSpec((None, None, None, past, HEAD_DIM), lambda b, h, i: (b, jl, h, 0, 0)),
        pl.BlockSpec((None, None, None, past, HEAD_DIM), lambda b, h, i: (b, jl, h, 0, 0)),
        pl.BlockSpec(memory_space=pltpu.SMEM),
        pl.BlockSpec((None, tq, q_w), lambda b, h, i: (b + b_off, i, h)),
    ]

    def body(q_ref, k_ref, v_ref, ck_ref, cv_ref, sink_ref, _o_in, o_ref):
        _window_kernel(q_ref, k_ref, v_ref, ck_ref, cv_ref, sink_ref, o_ref, tq=tq, n_h=n_h, seq_len=T)

    return pl.pallas_call(
        body,
        out_shape=jax.ShapeDtypeStruct(o_arr.shape, o_arr.dtype),
        grid=(n_b, C_KV_HEADS, T // tq),
        in_specs=in_specs,
        out_specs=pl.BlockSpec((None, tq, q_w), lambda b, h, i: (b + b_off, i, h)),
        input_output_aliases={6: 0},
        compiler_params=_params("arbitrary", "arbitrary", "arbitrary"),
        name="window_attention",
    )(p_arr, p_arr, p_arr, cache_k, cache_v, sink, o_arr)


def _mlstm_kernel(*refs, nc, has_init, emit_state):
    it = iter(refs)
    qf, kf, vf0, vf1, qb, kb, vb0, vb1 = [next(it) for _ in range(8)]
    gcf, grf, gcb, grb = [next(it) for _ in range(4)]
    if has_init:
        c0_ref, n0_ref, m0_ref = next(it), next(it), next(it)
    _hf_in, _hb_in = next(it), next(it)
    hf_ref, hb_ref = next(it), next(it)
    if emit_state:
        cout_ref, nout_ref, mout_ref = next(it), next(it), next(it)
    c_sc, n_sc, m_sc = next(it), next(it), next(it)
    j = pl.program_id(1)
    L = CHUNK
    nch = 2 * D_HEADS

    @pl.when(j == 0)
    def _():
        if has_init:
            c_sc[...] = c0_ref[...]
            n_sc[...] = n0_ref[...]
            m_sc[...] = m0_ref[...]
        else:
            c_sc[...] = jnp.zeros(c_sc.shape, jnp.float32)
            n_sc[...] = jnp.zeros(n_sc.shape, jnp.float32)
            m_sc[...] = jnp.zeros(m_sc.shape, jnp.float32)

    row = lax.broadcasted_iota(jnp.int32, (L, L), 0)
    col = lax.broadcasted_iota(jnp.int32, (L, L), 1)
    lower = col <= row
    upper = col >= row
    lower_m = jnp.where(lower, 1.0, 0.0).astype(_MXU)
    upper_m = jnp.where(upper, 1.0, 0.0).astype(_MXU)

    def exact_dot(a, b, a_is_data):
        if a_is_data:
            hi, lo = _split_hi_lo(a)
            return _dot(hi, b) + _dot(lo, b)
        hi, lo = _split_hi_lo(b)
        return _dot(a, hi) + _dot(a, lo)

    for d in range(2):
        rev = d == 1
        q_ref, k_ref, v_refs = (qb, kb, (vb0, vb1)) if rev else (qf, kf, (vf0, vf1))
        gc_ref, gr_ref = (gcb, grb) if rev else (gcf, grf)
        h_ref = hb_ref if rev else hf_ref
        gc = gc_ref[...]
        gr = gr_ref[...]
        lf_c = _log_sigmoid(gc)
        lf_r = _log_sigmoid(gr)
        b_cols = exact_dot(upper_m if rev else lower_m, lf_c, False)
        b_rows = exact_dot(lf_r, lower_m if rev else upper_m, True)
        tri = upper if rev else lower
        for hd in range(D_HEADS):
            ci = d * D_HEADS + hd
            ic, fc = d * 2 * D_HEADS + hd, d * 2 * D_HEADS + D_HEADS + hd
            ig_c, ig_r = gc[:, ic:ic + 1], gr[ic:ic + 1, :]
            b_c, b_r = b_cols[:, fc:fc + 1], b_rows[fc:fc + 1, :]
            q = q_ref[:, hd * D_QK:(hd + 1) * D_QK]
            k = k_ref[:, hd * D_QK:(hd + 1) * D_QK]
            v_ref = v_refs[hd // 2]
            v = v_ref[:, (hd % 2) * D_V:(hd % 2 + 1) * D_V]
            m = m_sc[ci:ci + 1, 0:1]
            n = n_sc[ci:ci + 1, :]
            c_mat = c_sc[ci]
            qm, km, vm = q.astype(_MXU), k.astype(_MXU), v.astype(_MXU)

            dmat = jnp.where(tri, b_c - b_r + ig_r, NEG_INF)
            m_prev = b_c + m
            m_t = jnp.maximum(m_prev, jnp.max(dmat, axis=-1, keepdims=True))
            w_intra = jnp.exp(dmat - m_t)
            w_prev = jnp.exp(m_prev - m_t)
            s = _dot_nt(qm, km) * w_intra
            num = w_prev * _dot(qm, c_mat.astype(_MXU)) + _dot(s.astype(_MXU), vm)
            den = w_prev * jnp.sum(q * n, axis=-1, keepdims=True) + jnp.sum(s, axis=-1, keepdims=True)
            h = num / jnp.maximum(jnp.abs(den), jnp.exp(-m_t))
            h_ref[:, hd * D_V:(hd + 1) * D_V] = h.astype(h_ref.dtype)

            b_end = b_c[0:1] if rev else b_c[L - 1:L]
            g_c = b_end - b_c + ig_c
            g_r = b_end - b_r + ig_r
            m_new = jnp.maximum(b_end + m, jnp.max(g_r, axis=-1, keepdims=True))
            wk = jnp.exp(g_c - m_new)
            decay = jnp.exp(b_end + m - m_new)
            c_sc[ci] = decay * c_mat + _dot_tn(km, (wk * v).astype(_MXU))
            n_sc[ci:ci + 1, :] = decay * n + jnp.sum(wk * k, axis=0, keepdims=True)
            m_sc[ci:ci + 1, :] = jnp.broadcast_to(m_new, (1, LANES))

    if emit_state:
        @pl.when(j == nc - 1)
        def _():
            cout_ref[...] = c_sc[...]
            nout_ref[...] = n_sc[...]
            mout_ref[...] = m_sc[...]


def _mlstm(p_arr, gates, gates_t, hf_arr, hb_arr, *, n_seq, seq_len, b_off, q_col, k_col, v_col,
           init=None, emit_state=False):
    G, T, _ = p_arr.shape
    per_b = T // seq_len
    nc = seq_len // CHUNK
    wq = D_HEADS * D_QK
    hw = D_HEADS * D_V
    nch = 2 * D_HEADS

    def rows(s, j, rev):
        jj = nc - 1 - j if rev else j
        return b_off + s // per_b, (s % per_b) * nc + jj

    def col_spec(width, col, rev):
        return pl.BlockSpec((None, CHUNK, width), lambda s, j: rows(s, j, rev) + (col // width,))

    def gate_t_spec(rev):
        def imap(s, j):
            b, r = rows(s, j, rev)
            return b, 0, r
        return pl.BlockSpec((None, N_GATES, CHUNK), imap)

    in_specs, args = [], []
    for rev in (False, True):
        in_specs += [col_spec(wq, q_col, rev), col_spec(wq, k_col, rev),
                     col_spec(wq, v_col, rev), col_spec(wq, v_col + wq, rev)]
        args += [p_arr] * 4
    for rev in (False, True):
        in_specs += [col_spec(LANES, 0, rev), gate_t_spec(rev)]
        args += [gates, gates_t]
    if init is not None:
        c0, n0, m0, jl = init
        in_specs += [pl.BlockSpec((None, None, nch, D_QK, D_V), lambda s, j: (s, jl, 0, 0, 0)),
                     pl.BlockSpec((None, None, nch, D_QK), lambda s, j: (s, jl, 0, 0)),
                     pl.BlockSpec((None, None, nch, LANES), lambda s, j: (s, jl, 0, 0))]
        args += [c0, n0, m0]
    n_in = len(args)
    h_specs = [col_spec(hw, 0, False), col_spec(hw, 0, True)]
    in_specs += h_specs
    args += [hf_arr, hb_arr]
    out_shape = [jax.ShapeDtypeStruct(hf_arr.shape, hf_arr.dtype),
                 jax.ShapeDtypeStruct(hb_arr.shape, hb_arr.dtype)]
    out_specs = list(h_specs)
    if emit_state:
        out_shape += [jax.ShapeDtypeStruct((n_seq, nch, D_QK, D_V), jnp.float32),
                      jax.ShapeDtypeStruct((n_seq, nch, D_QK), jnp.float32),
                      jax.ShapeDtypeStruct((n_seq, nch, LANES), jnp.float32)]
        out_specs += [pl.BlockSpec((None, nch, D_QK, D_V), lambda s, j: (s, 0, 0, 0)),
                      pl.BlockSpec((None, nch, D_QK), lambda s, j: (s, 0, 0)),
                      pl.BlockSpec((None, nch, LANES), lambda s, j: (s, 0, 0))]
    return pl.pallas_call(
        functools.partial(_mlstm_kernel, nc=nc, has_init=init is not None, emit_state=emit_state),
        out_shape=out_shape,
        grid=(n_seq, nc),
        in_specs=in_specs,
        out_specs=out_specs,
        scratch_shapes=[pltpu.VMEM((nch, D_QK, D_V), jnp.float32), pltpu.VMEM((nch, D_QK), jnp.float32),
                        pltpu.VMEM((nch, LANES), jnp.float32)],
        input_output_aliases={n_in: 0, n_in + 1: 1},
        compiler_params=_params("arbitrary", "arbitrary"),
        name="mlstm",
    )(*args)


def _outproj_kernel(*refs, odd, nj, tn, n_experts):
    it = iter(refs)
    if odd:
        oc_ref, hf_ref, hb_ref, do0_ref, do1_ref, dng_ref = [next(it) for _ in range(6)]
    else:
        o_ref = next(it)
    x_ref, g1_ref, w_ref, ng_ref, sh2_ref, sc2_ref, rwh_ref, rwl_ref, rb_ref = [next(it) for _ in range(9)]
    xn_ref, f_ref, ti_ref, tg_ref, cnt_ref = [next(it) for _ in range(5)]
    xn_sc, cnt_sc = next(it), next(it)
    om_sc = next(it) if odd else None
    j = pl.program_id(2)
    first_tile = (pl.program_id(0) == 0) & (pl.program_id(1) == 0)

    if odd:
        @pl.when(j == 0)
        def _():
            wc = oc_ref.shape[1]
            om_sc[:, :wc] = oc_ref[...].astype(_MXU)
            hd = hf_ref[...] + hb_ref[...]
            half = do0_ref.shape[1]
            for hh in range(D_HEADS):
                y = hd[:, hh * D_V:(hh + 1) * D_V]
                y = y * lax.rsqrt(jnp.mean(y * y, axis=-1, keepdims=True) + EPS) * dng_ref[:, hh * D_V:(hh + 1) * D_V]
                do_ref = do0_ref if hh * D_V < half else do1_ref
                c0 = (hh * D_V) % half
                y = y * _sigmoid(do_ref[:, c0:c0 + D_V])
                om_sc[:, wc + hh * D_V:wc + (hh + 1) * D_V] = y.astype(_MXU)
        o = om_sc[...]
    else:
        o = o_ref[...].astype(_MXU)

    xn = x_ref[...] + g1_ref[...] * _dot(o, w_ref[...].astype(_MXU))
    xn_ref[...] = xn
    xn_sc[j] = xn

    @pl.when(j == nj - 1)
    def _():
        ss = jnp.zeros((xn.shape[0], 1), jnp.float32)
        for c in range(nj):
            t = xn_sc[c]
            ss = ss + jnp.sum(t * t, axis=-1, keepdims=True)
        inv = lax.rsqrt(ss / (nj * tn) + EPS)
        logits = rb_ref[...]
        for c in range(nj):
            cs = slice(c * tn, (c + 1) * tn)
            f = (xn_sc[c] * inv * ng_ref[:, cs]) * (1.0 + sc2_ref[:, cs]) + sh2_ref[:, cs]
            xn_sc[c] = f
            fh, fl = _split_hi_lo(f)
            logits = logits + (_dot(fh, rwh_ref[cs, :]) + _dot(fl, rwh_ref[cs, :]) + _dot(fh, rwl_ref[cs, :]))
        half = nj * tn // 2
        pw = min(tn, half)
        for pc in range(half // pw):
            lo_c, hi_c = pc * pw, half + pc * pw
            lo = xn_sc[lo_c // tn][:, lo_c % tn:lo_c % tn + pw]
            hi = xn_sc[hi_c // tn][:, hi_c % tn:hi_c % tn + pw]
            f_ref[:, pc * pw:(pc + 1) * pw] = _pack_bf16_pair(lo, hi)
        lane = lax.broadcasted_iota(jnp.int32, logits.shape, 1)
        cur = logits
        vals, idxs = [], []
        for _ in range(TOP_K):
            mx = jnp.max(cur, axis=-1, keepdims=True)
            ix = jnp.min(jnp.where(cur == mx, lane, LANES), axis=-1, keepdims=True)
            vals.append(mx)
            idxs.append(ix)
            cur = jnp.where(lane == ix, -jnp.inf, cur)
        es = [jnp.exp(v - vals[0]) for v in vals]
        tot = es[0] + es[1] + es[2] + es[3]
        ti = jnp.zeros(logits.shape, jnp.int32)
        tg = jnp.zeros(logits.shape, jnp.float32)
        for kk in range(TOP_K):
            ti = jnp.where(lane == kk, idxs[kk], ti)
            tg = jnp.where(lane == kk, es[kk] / tot, tg)
        tg_ref[...] = tg

        @pl.when(first_tile)
        def _():
            cnt_sc[...] = jnp.zeros(cnt_sc.shape, jnp.float32)

        tm = logits.shape[0]
        onehot = [jnp.where(lane == idxs[kk], 1.0, 0.0) for kk in range(TOP_K)]
        picked = functools.reduce(jnp.add, onehot)
        r_i = lax.broadcasted_iota(jnp.int32, (tm, tm), 0)
        c_i = lax.broadcasted_iota(jnp.int32, (tm, tm), 1)
        earlier = jnp.where(c_i < r_i, 1.0, 0.0).astype(_MXU)
        before = cnt_sc[...] + _dot(earlier, picked.astype(_MXU))
        for kk in range(TOP_K):
            rank = jnp.sum(onehot[kk] * before, axis=-1, keepdims=True)
            ti = jnp.where(lane == TOP_K + kk, rank.astype(jnp.int32), ti)
        ti_ref[...] = ti
        cnt_sc[...] += jnp.sum(picked, axis=0, keepdims=True)
        cnt_ref[...] = cnt_sc[...]


def _out_project(x, mods, layer, w_out, norm_ffn_g, rwh, rwl, rb, n_experts, *, o=None, odd_in=None):
    G, T, D = x.shape
    mix = w_out.shape[1]
    tm = min(512, T)
    tn = _tile(D, 512)
    nj = D // tn
    odd = odd_in is not None
    bmap = lambda b, i, j: (b, i, 0)
    if odd:
        oc, hf, hb, p_cd, do_col, d_norm_g = odd_in
        half = (D_HEADS * D_V) // 2
        in_specs = [pl.BlockSpec((None, tm, oc.shape[-1]), bmap),
                    pl.BlockSpec((None, tm, hf.shape[-1]), bmap),
                    pl.BlockSpec((None, tm, hb.shape[-1]), bmap),
                    pl.BlockSpec((None, tm, half), lambda b, i, j: (b, i, do_col // half)),
                    pl.BlockSpec((None, tm, half), lambda b, i, j: (b, i, do_col // half + 1)),
                    pl.BlockSpec((1, D_HEADS * D_V), lambda b, i, j: (0, 0))]
        args = [oc, hf, hb, p_cd, p_cd, d_norm_g]
    else:
        in_specs = [pl.BlockSpec((None, tm, mix), bmap)]
        args = [o]
    in_specs += [
        pl.BlockSpec((None, tm, tn), lambda b, i, j: (b, i, j)),
        pl.BlockSpec((None, None, None, 1, tn), lambda b, i, j: (layer, 2, b, 0, j)),
        pl.BlockSpec((None, mix, tn), lambda b, i, j: (layer, 0, j)),
        pl.BlockSpec((None, 1, D), lambda b, i, j: (layer, 0, 0)),
        pl.BlockSpec((None, None, None, 1, D), lambda b, i, j: (layer, 3, b, 0, 0)),
        pl.BlockSpec((None, None, None, 1, D), lambda b, i, j: (layer, 4, b, 0, 0)),
        pl.BlockSpec((None, D, LANES), lambda b, i, j: (layer, 0, 0)),
        pl.BlockSpec((None, D, LANES), lambda b, i, j: (layer, 0, 0)),
        pl.BlockSpec((None, 1, LANES), lambda b, i, j: (layer, 0, 0)),
    ]
    args += [x, mods, w_out, norm_ffn_g.reshape(norm_ffn_g.shape[0], 1, D), mods, mods, rwh, rwl, rb]
    out_shape = [jax.ShapeDtypeStruct((G, T, D), jnp.float32),
                 jax.ShapeDtypeStruct((G, T, D // 2), jnp.uint32),
                 jax.ShapeDtypeStruct((G, T, LANES), jnp.int32),
                 jax.ShapeDtypeStruct((G, T, LANES), jnp.float32),
                 jax.ShapeDtypeStruct((1, LANES), jnp.float32)]
    out_specs = [pl.BlockSpec((None, tm, tn), lambda b, i, j: (b, i, j)),
                 pl.BlockSpec((None, tm, D // 2), bmap),
                 pl.BlockSpec((None, tm, LANES), bmap),
                 pl.BlockSpec((None, tm, LANES), bmap),
                 pl.BlockSpec((1, LANES), lambda b, i, j: (0, 0))]
    scratch = [pltpu.VMEM((nj, tm, tn), jnp.float32), pltpu.VMEM((1, LANES), jnp.float32)]
    if odd:
        scratch.append(pltpu.VMEM((tm, mix), _MXU))
    return pl.pallas_call(
        functools.partial(_outproj_kernel, odd=odd, nj=nj, tn=tn, n_experts=n_experts),
        out_shape=out_shape,
        grid=(G, T // tm, nj),
        in_specs=in_specs,
        out_specs=out_specs,
        scratch_shapes=scratch,
        compiler_params=_params("arbitrary", "arbitrary", "arbitrary"),
        name="out_project_router",
    )(*args)


def _moe_kernel(te_ref, ns_ref, x_ref, wg_ref, wu_ref, wd_ref, bg_ref, bu_ref, bd_ref, o_ref, *, n_sub):
    t = pl.program_id(0)
    c = pl.program_id(1)
    nsub = ns_ref[t]
    half = x_ref.shape[1]

    def expert_rows(rs):
        lo, hi = _unpack_bf16_pair(x_ref[rs, :])
        gt = (_dot(lo, wg_ref[:half, :].astype(_MXU)) + _dot(hi, wg_ref[half:, :].astype(_MXU))) + bg_ref[...]
        up = (_dot(lo, wu_ref[:half, :].astype(_MXU)) + _dot(hi, wu_ref[half:, :].astype(_MXU))) + bu_ref[...]
        gt = jnp.minimum(gt, SWIGLU_LIMIT)
        up = jnp.clip(up, -SWIGLU_LIMIT, SWIGLU_LIMIT)
        act = gt * _sigmoid(SWIGLU_ALPHA * gt) * (up + 1.0)
        part = _dot(act.astype(_MXU), wd_ref[...].astype(_MXU))

        @pl.when(c == 0)
        def _():
            o_ref[rs, :] = part + bd_ref[...]

        @pl.when(c > 0)
        def _():
            o_ref[rs, :] += part

    @pl.when(nsub == n_sub)
    def _():
        expert_rows(slice(0, n_sub * MOE_SUB))

    for s in range(n_sub):
        rs = slice(s * MOE_SUB, (s + 1) * MOE_SUB)

        @pl.when((s < nsub) & (nsub < n_sub))
        def _(rs=rs):
            expert_rows(rs)

        @pl.when((s >= nsub) & (c == 0))
        def _(rs=rs):
            o_ref[rs, :] = jnp.zeros((MOE_SUB, o_ref.shape[1]), jnp.float32)


def _moe_experts(xb, tile_expert, tile_nsub, w_gate_up, b_gate_up, w_down, b_down, layer):
    P, half = xb.shape
    E, D, F2 = w_gate_up.shape[1:]
    F = F2 // 2
    fc = _tile(F, 512)
    nfc = F // fc
    n_tiles = P // MOE_TILE
    grid_spec = pltpu.PrefetchScalarGridSpec(
        num_scalar_prefetch=2,
        grid=(n_tiles, nfc),
        in_specs=[
            pl.BlockSpec((MOE_TILE, half), lambda t, c, te, ns: (t, 0)),
            pl.BlockSpec((None, None, D, fc), lambda t, c, te, ns: (layer, te[t], 0, c)),
            pl.BlockSpec((None, None, D, fc), lambda t, c, te, ns: (layer, te[t], 0, nfc + c)),
            pl.BlockSpec((None, None, fc, D), lambda t, c, te, ns: (layer, te[t], c, 0)),
            pl.BlockSpec((None, None, 1, fc), lambda t, c, te, ns: (layer, te[t], 0, c)),
            pl.BlockSpec((None, None, 1, fc), lambda t, c, te, ns: (layer, te[t], 0, nfc + c)),
            pl.BlockSpec((None, None, 1, D), lambda t, c, te, ns: (layer, te[t], 0, 0)),
        ],
        out_specs=pl.BlockSpec((MOE_TILE, D), lambda t, c, te, ns: (t, 0)),
    )
    depth = w_gate_up.shape[0]
    return pl.pallas_call(
        functools.partial(_moe_kernel, n_sub=MOE_TILE // MOE_SUB),
        out_shape=jax.ShapeDtypeStruct((P, D), jnp.float32),
        grid_spec=grid_spec,
        compiler_params=_params("arbitrary", "arbitrary"),
        name="moe_experts",
    )(tile_expert, tile_nsub, xb, w_gate_up, w_gate_up, w_down,
      b_gate_up.reshape(depth, E, 1, F2), b_gate_up.reshape(depth, E, 1, F2), b_down.reshape(depth, E, 1, D))


def _dispatch_kernel(pos_ref, f_ref, _xb_in, xb_ref, sem, *, tt):
    def issue(r, carry):
        for k in range(TOP_K):
            dst = pos_ref[0, r * TOP_K + k]
            pltpu.make_async_copy(f_ref.at[pl.ds(r, 1)], xb_ref.at[pl.ds(dst, 1)], sem).start()
        return carry

    lax.fori_loop(0, tt, issue, 0)
    for k in range(TOP_K):
        pltpu.make_async_copy(f_ref, f_ref, sem).wait()


def _dispatch(f_packed, pos, n_rows):
    N, half = f_packed.shape
    tt = min(512, N)
    xb0 = jnp.zeros((n_rows, half), f_packed.dtype)
    return pl.pallas_call(
        functools.partial(_dispatch_kernel, tt=tt),
        out_shape=jax.ShapeDtypeStruct((n_rows, half), f_packed.dtype),
        grid=(N // tt,),
        in_specs=[pl.BlockSpec((None, 1, tt * TOP_K), lambda i: (i, 0, 0), memory_space=pltpu.SMEM),
                  pl.BlockSpec((tt, half), lambda i: (i, 0)),
                  pl.BlockSpec(memory_space=pl.ANY)],
        out_specs=pl.BlockSpec(memory_space=pl.ANY),
        scratch_shapes=[pltpu.SemaphoreType.DMA],
        input_output_aliases={2: 0},
        compiler_params=_params("arbitrary"),
        name="moe_dispatch",
    )(pos.reshape(N // tt, 1, tt * TOP_K), f_packed, xb0)


def _combine_kernel(pos_ref, posn_ref, gate_ref, x_ref, g2_ref, *rest, tt, n_steps, final):
    if final:
        ng_ref, yb_ref, o_ref, buf, sem = rest
    else:
        yb_ref, o_ref, buf, sem = rest
    i = pl.program_id(0)
    slot = i % 2

    def gather(p_ref, sl):
        def issue(r, carry):
            for k in range(TOP_K):
                src = p_ref[0, r * TOP_K + k]
                pltpu.make_async_copy(yb_ref.at[pl.ds(src, 1)], buf.at[sl, k, pl.ds(r, 1)], sem.at[sl]).start()
            return carry
        lax.fori_loop(0, tt, issue, 0)

    @pl.when(i == 0)
    def _():
        gather(pos_ref, 0)

    @pl.when(i + 1 < n_steps)
    def _():
        gather(posn_ref, 1 - slot)

    for k in range(TOP_K):
        pltpu.make_async_copy(buf.at[slot, k], buf.at[slot, k], sem.at[slot]).wait()
    gate = gate_ref[...]
    y = gate[:, 0:1] * buf[slot, 0]
    for k in range(1, TOP_K):
        y = y + gate[:, k:k + 1] * buf[slot, k]
    xn = x_ref[...] + g2_ref[...] * y
    if final:
        xn = xn * lax.rsqrt(jnp.mean(xn * xn, axis=-1, keepdims=True) + EPS) * ng_ref[...]
    o_ref[...] = xn


def _combine(x, yb, pos, top_gate, mods, layer, final_g):
    G, T, D = x.shape
    N = G * T
    tt = min(256, T)
    per_b = T // tt
    n_steps = N // tt
    final = final_g is not None
    pos3 = pos.reshape(n_steps, 1, tt * TOP_K)
    in_specs = [
        pl.BlockSpec((None, 1, tt * TOP_K), lambda i: (i, 0, 0), memory_space=pltpu.SMEM),
        pl.BlockSpec((None, 1, tt * TOP_K), lambda i: (jnp.minimum(i + 1, n_steps - 1), 0, 0),
                     memory_space=pltpu.SMEM),
        pl.BlockSpec((tt, LANES), lambda i: (i, 0)),
        pl.BlockSpec((tt, D), lambda i: (i, 0)),
        pl.BlockSpec((None, None, None, 1, D), lambda i: (layer, 5, i // per_b, 0, 0)),
    ]
    args = [pos3, pos3, top_gate, x.reshape(N, D), mods]
    if final:
        in_specs.append(pl.BlockSpec((1, D), lambda i: (0, 0)))
        args.append(final_g.reshape(1, D))
    in_specs.append(pl.BlockSpec(memory_space=pl.ANY))
    args.append(yb)
    out = pl.pallas_call(
        functools.partial(_combine_kernel, tt=tt, n_steps=n_steps, final=final),
        out_shape=jax.ShapeDtypeStruct((N, D), jnp.float32),
        grid=(n_steps,),
        in_specs=in_specs,
        out_specs=pl.BlockSpec((tt, D), lambda i: (i, 0)),
        scratch_shapes=[pltpu.VMEM((2, TOP_K, tt, D), jnp.float32), pltpu.SemaphoreType.DMA((2,))],
        compiler_params=_params("arbitrary"),
        name="moe_combine",
    )(*args)
    return out.reshape(G, T, D)


def _moe_layer(x, f_packed, top_i, top_g, cnt, mods, layer, w_gate_up, b_gate_up, w_down, b_down, final_g):
    G, T, D = x.shape
    N = G * T
    E = w_gate_up.shape[1]
    counts = cnt[0, :E].astype(jnp.int32)
    padded = (counts + MOE_TILE - 1) // MOE_TILE * MOE_TILE
    pad_end = jnp.cumsum(padded)
    pad_start = pad_end - padded
    ti = top_i.reshape(N, LANES)
    top_e, rank = ti[:, :TOP_K], ti[:, TOP_K:2 * TOP_K]
    start_of = jnp.sum(jnp.where(top_e[:, :, None] == jnp.arange(E, dtype=jnp.int32), pad_start, 0), axis=-1)
    pos = (start_of + rank).astype(jnp.int32)
    n_tiles = (N * TOP_K + MOE_TILE - 1) // MOE_TILE + E
    tile_row0 = jnp.arange(n_tiles, dtype=jnp.int32) * MOE_TILE
    tile_expert = jnp.minimum(jnp.searchsorted(pad_end, tile_row0, side="right"), E - 1).astype(jnp.int32)
    valid = jnp.clip(counts[tile_expert] - (tile_row0 - pad_start[tile_expert]), 0, MOE_TILE)
    valid = jnp.where(tile_row0 < pad_end[-1], valid, 0)
    tile_nsub = ((valid + MOE_SUB - 1) // MOE_SUB).astype(jnp.int32)
    xb = _dispatch(f_packed.reshape(N, D // 2), pos, n_tiles * MOE_TILE)
    yb = _moe_experts(xb, tile_expert, tile_nsub, w_gate_up, b_gate_up, w_down, b_down, layer)
    return _combine(x, yb, pos, top_g.reshape(N, LANES), mods, layer, final_g)


def _rope_tables(T):
    n_rows = T // GRID_W
    row = jnp.repeat(jnp.arange(n_rows, dtype=jnp.float32), GRID_W)
    col = jnp.tile(jnp.arange(GRID_W, dtype=jnp.float32), n_rows)
    inv = ROPE_THETA ** (-jnp.arange(0, AXIS_DIM, 2, dtype=jnp.float32) / AXIS_DIM)
    cr, sr = jnp.cos(row[:, None] * inv), jnp.sin(row[:, None] * inv)
    cc, sc = jnp.cos(col[:, None] * inv), jnp.sin(col[:, None] * inv)
    cos_t = jnp.concatenate([cr, cr, cc, cc], axis=-1)
    sin_t = jnp.concatenate([-sr, sr, -sc, sc], axis=-1)
    return (jnp.stack([jnp.ones_like(cos_t), cos_t]), jnp.stack([jnp.zeros_like(sin_t), sin_t]))


def _heads_out(p0, col, n, d, Bp, Tp):
    return p0[:, col:col + n * d].reshape(Bp, Tp, n, d).transpose(0, 2, 1, 3)


def kernel(x_prompt, x_sample, cache_a_k, cache_a_v, cache_b_k, cache_b_v, cache_c_k, cache_c_v, state_d_C, state_d_n, state_d_m, c, c_ctx, norm_mix_g, norm_ffn_g, w_mod, b_mod, w_out, w_in_ab, a_q_norm_g, a_k_norm_g, b_lambda, b_subln_g, w_in_cd, b_gates, c_sink, d_norm_g, router_w, router_b, w_gate_up, b_gate_up, w_down, b_down, final_norm_g):
    Bp, Tp, D = x_prompt.shape
    Bs, T, _ = x_sample.shape
    assert Bp * Tp == T, "context tokens are laid out as one extra batch of DEC_SEQ tokens"
    depth = w_mod.shape[0]
    E = router_w.shape[-1]
    past = cache_a_k.shape[-2]
    G = 1 + Bs
    scale = HEAD_DIM ** -0.5
    f32 = jnp.float32

    x = jnp.concatenate([x_prompt.reshape(1, T, D), x_sample], axis=0)
    cond = jnp.concatenate([c_ctx[None], c, jnp.zeros((8 - G, D), f32)], axis=0)
    mods = _modulation(cond, w_mod, b_mod)
    mods = mods[:, :G].reshape(depth, G, 6, 1, D).transpose(0, 2, 1, 3, 4)

    cos_t, sin_t = _rope_tables(T)
    pad_e = LANES - E
    rw = jnp.pad(router_w, ((0, 0), (0, 0), (0, pad_e)))
    rwh = rw.astype(_MXU)
    rwl = (rw - rwh.astype(f32)).astype(_MXU)
    rb = jnp.pad(router_b, ((0, 0), (0, pad_e)), constant_values=NEG_INF).reshape(depth, 1, LANES)

    ab_ops = ([("q", True, scale)] * A_HEADS + [("k", True, 1.0)] * A_KV_HEADS + [(None, False, 1.0)] * A_KV_HEADS
              + [(None, True, scale)] * (2 * B_HEADS) + [(None, True, 1.0)] * (2 * B_HEADS)
              + [(None, False, 1.0)] * (B_HEADS * B_VDIM // HEAD_DIM))
    cd_ops = ([(None, True, scale)] * C_HEADS + [(None, True, 1.0)] * C_KV_HEADS + [(None, False, 1.0)] * C_KV_HEADS
              + [(None, False, 1.0)] * D_HEADS + [(None, False, D_QK ** -0.5)] * D_HEADS
              + [(None, False, 1.0)] * (2 * D_HEADS * D_V // HEAD_DIM))
    A_Q, A_K, A_V = 0, A_HEADS * HEAD_DIM, (A_HEADS + A_KV_HEADS) * HEAD_DIM
    B_Q = (A_HEADS + 2 * A_KV_HEADS) * HEAD_DIM
    B_K = B_Q + 2 * B_HEADS * HEAD_DIM
    B_V = B_K + 2 * B_HEADS * HEAD_DIM
    C_Q, C_K, C_V = 0, C_HEADS * HEAD_DIM, (C_HEADS + C_KV_HEADS) * HEAD_DIM
    D_Q = (C_HEADS + 2 * C_KV_HEADS) * HEAD_DIM
    D_K = D_Q + D_HEADS * D_QK
    D_VC = D_K + D_HEADS * D_QK
    D_O = D_VC + D_HEADS * D_V
    a_grp = A_HEADS // A_KV_HEADS
    c_grp = C_HEADS // C_KV_HEADS
    ones_g = jnp.ones((1, HEAD_DIM), f32)

    new = {k: [] for k in ("ak", "av", "bk", "bv", "ck", "cv", "dC", "dn", "dm")}

    for layer in range(depth):
        jl = layer // 2
        if layer % 2 == 0:
            lam_init = 0.8 - 0.6 * math.exp(-0.3 * layer)
            p = _project(x, mods, layer, norm_mix_g, w_in_ab, jl, AB_WIDTH, ab_ops, cos_t, sin_t,
                         a_q_norm_g[jl].reshape(1, HEAD_DIM), a_k_norm_g[jl].reshape(1, HEAD_DIM))
            p0 = p[0]
            new["ak"].append(_heads_out(p0, A_K, A_KV_HEADS, HEAD_DIM, Bp, Tp))
            new["av"].append(_heads_out(p0, A_V, A_KV_HEADS, HEAD_DIM, Bp, Tp))
            new["bk"].append(p0[:, B_K:B_V].reshape(Bp, Tp, B_HEADS, 2, HEAD_DIM).transpose(0, 3, 2, 1, 4))
            new["bv"].append(_heads_out(p0, B_V, B_HEADS, B_VDIM, Bp, Tp))

            o = jnp.zeros((G, T, MIX_WIDTH), _MXU)
            a_groups = ((tuple(t * HEAD_DIM for t in range(a_grp)), 0),)
            b_groups = (((0,), 0), ((HEAD_DIM,), HEAD_DIM))
            diff = (b_lambda[jl], b_subln_g[jl].reshape(1, B_VDIM), lam_init)
            a_common = dict(n_kvh=A_KV_HEADS, groups=a_groups, q_col=A_Q, q_w=a_grp * HEAD_DIM, k_col=A_K,
                            k_w=HEAD_DIM, v_col=A_V, dv=HEAD_DIM, o_col=0, o_w=a_grp * HEAD_DIM)
            b_common = dict(n_kvh=B_HEADS, groups=b_groups, q_col=B_Q, q_w=2 * HEAD_DIM, k_col=B_K,
                            k_w=2 * HEAD_DIM, v_col=B_V, dv=B_VDIM, o_col=A_HEADS * HEAD_DIM, o_w=B_VDIM, diff=diff)
            o = _attention(p, o, n_seq=Bp, seq_len=Tp, b_off=0, tq=min(256, Tp), name="ctx_attn_a", **a_common)
            o = _attention(p, o, n_seq=Bp, seq_len=Tp, b_off=0, tq=min(256, Tp), name="ctx_attn_b", **b_common)
            cache_a = ([(cache_a_k, pl.BlockSpec((None, None, None, past, HEAD_DIM),
                                                  lambda s, h, i: (s, jl, h, 0, 0)))],
                       (cache_a_v, pl.BlockSpec((None, None, None, past, HEAD_DIM),
                                                lambda s, h, i: (s, jl, h, 0, 0))))
            cache_b = ([(cache_b_k, pl.BlockSpec((None, None, None, None, past, HEAD_DIM),
                                                  lambda s, h, i, mm=mm: (s, jl, mm, h, 0, 0))) for mm in range(2)],
                       (cache_b_v, pl.BlockSpec((None, None, None, past, B_VDIM),
                                                lambda s, h, i: (s, jl, h, 0, 0))))
            o = _attention(p, o, n_seq=Bs, seq_len=T, b_off=1, tq=min(256, T), cache=cache_a,
                           name="latent_attn_a", **a_common)
            o = _attention(p, o, n_seq=Bs, seq_len=T, b_off=1, tq=min(512, T), cache=cache_b,
                           name="latent_attn_b", **b_common)
            x, f_tok, top_i, top_g, cnt = _out_project(x, mods, layer, w_out, norm_ffn_g, rwh, rwl, rb, E, o=o)
        else:
            wg = jnp.pad(w_in_cd[jl][:, CD_MAIN:], ((0, 0), (0, LANES - N_GATES)))
            bg = jnp.pad(b_gates[jl], (0, LANES - N_GATES)).reshape(1, LANES)
            p, gates = _project(x, mods, layer, norm_mix_g, w_in_cd, jl, CD_MAIN, cd_ops, cos_t, sin_t,
                                ones_g, ones_g, gates=(wg, bg))
            gates_t = jnp.swapaxes(gates[:, :, :N_GATES], 1, 2)
            p0 = p[0]
            new["ck"].append(_heads_out(p0, C_K, C_KV_HEADS, HEAD_DIM, Bp, Tp))
            new["cv"].append(_heads_out(p0, C_V, C_KV_HEADS, HEAD_DIM, Bp, Tp))

            oc = jnp.zeros((G, T, C_HEADS * HEAD_DIM), _MXU)
            c_groups = ((tuple(t * HEAD_DIM for t in range(c_grp)), 0),)
            oc = _attention(p, oc, n_seq=Bp, seq_len=Tp, b_off=0, tq=min(256, Tp), n_kvh=C_KV_HEADS,
                            groups=c_groups, q_col=C_Q, q_w=c_grp * HEAD_DIM, k_col=C_K, k_w=HEAD_DIM,
                            v_col=C_V, dv=HEAD_DIM, o_col=0, o_w=c_grp * HEAD_DIM, sink=c_sink[jl],
                            name="ctx_attn_c")
            oc = _window_attention(p, oc, cache_c_k, cache_c_v, jl, c_sink[jl], b_off=1, n_b=Bs,
                                   q_col=C_Q, k_col=C_K, v_col=C_V)
            hf = jnp.zeros((G, T, D_HEADS * D_V), f32)
            hb = jnp.zeros((G, T, D_HEADS * D_V), f32)
            hf, hb, dC, dn, dm = _mlstm(p, gates, gates_t, hf, hb, n_seq=Bp, seq_len=Tp, b_off=0,
                                        q_col=D_Q, k_col=D_K, v_col=D_VC, emit_state=True)
            new["dC"].append(dC.reshape(Bp, 2, D_HEADS, D_QK, D_V))
            new["dn"].append(dn.reshape(Bp, 2, D_HEADS, D_QK))
            new["dm"].append(dm[:, :, 0].reshape(Bp, 2, D_HEADS))
            n_odd = state_d_C.shape[1]
            init = (state_d_C.reshape(Bs, n_odd, 2 * D_HEADS, D_QK, D_V),
                    state_d_n.reshape(Bs, n_odd, 2 * D_HEADS, D_QK),
                    jnp.broadcast_to(state_d_m.reshape(Bs, n_odd, 2 * D_HEADS, 1), (Bs, n_odd, 2 * D_HEADS, LANES)),
                    jl)
            hf, hb = _mlstm(p, gates, gates_t, hf, hb, n_seq=Bs, seq_len=T, b_off=1,
                            q_col=D_Q, k_col=D_K, v_col=D_VC, init=init)
            x, f_tok, top_i, top_g, cnt = _out_project(
                x, mods, layer, w_out, norm_ffn_g, rwh, rwl, rb, E,
                odd_in=(oc, hf, hb, p, D_O, d_norm_g[jl].reshape(1, D_HEADS * D_V)))

        x = _moe_layer(x, f_tok, top_i, top_g, cnt, mods, layer, w_gate_up, b_gate_up, w_down, b_down,
                       final_norm_g if layer == depth - 1 else None)

    y_prompt = x[0].reshape(Bp, Tp, D)
    y_sample = x[1:]
    return (y_prompt, y_sample,
            jnp.stack(new["ak"], 1), jnp.stack(new["av"], 1), jnp.stack(new["bk"], 1), jnp.stack(new["bv"], 1),
            jnp.stack(new["ck"], 1), jnp.stack(new["cv"], 1), jnp.stack(new["dC"], 1), jnp.stack(new["dn"], 1),
            jnp.stack(new["dm"], 1))
```

```python
import functools
import math

import jax
import jax.numpy as jnp
from jax import lax
from jax.experimental import pallas as pl
from jax.experimental.pallas import tpu as pltpu

HEAD_DIM = 128
GRID_W = 64
AXIS_DIM = HEAD_DIM // 2
ROPE_THETA = 10000.0
A_HEADS = 8
A_KV_HEADS = 2
B_HEADS = 4
B_VDIM = 2 * HEAD_DIM
C_HEADS = 8
C_KV_HEADS = 2
WINDOW = 128
D_HEADS = 4
D_QK = HEAD_DIM
D_V = 2 * HEAD_DIM
CHUNK = 128
MIX_WIDTH = A_HEADS * HEAD_DIM + B_HEADS * B_VDIM
AB_WIDTH = A_HEADS * HEAD_DIM + 2 * A_KV_HEADS * HEAD_DIM + 2 * B_HEADS * 2 * HEAD_DIM + B_HEADS * B_VDIM
CD_MAIN = C_HEADS * HEAD_DIM + 2 * C_KV_HEADS * HEAD_DIM + 2 * D_HEADS * D_QK + 2 * D_HEADS * D_V
N_GATES = 4 * D_HEADS
TOP_K = 4
SWIGLU_LIMIT = 7.0
SWIGLU_ALPHA = 1.702
EPS = 1e-6
NEG_INF = -1e30
LOG2E = math.log2(math.e)
LANES = 128
MOE_TILE = 1024
MOE_SUB = 256
VMEM_LIMIT = 56 * 1024 * 1024

_MXU = jnp.bfloat16


def _dot(a, b):
    return jnp.dot(a, b, preferred_element_type=jnp.float32)


def _dot_nt(a, b):
    return lax.dot_general(a, b, (((1,), (1,)), ((), ())), preferred_element_type=jnp.float32)


def _dot_tn(a, b):
    return lax.dot_general(a, b, (((0,), (0,)), ((), ())), preferred_element_type=jnp.float32)


def _split_hi_lo(a):
    hi = a.astype(_MXU)
    lo = (a - hi.astype(jnp.float32)).astype(_MXU)
    return hi, lo


def _pack_bf16_pair(lo, hi):
    lo_b = lax.bitcast_convert_type(lo.astype(jnp.bfloat16).astype(jnp.float32), jnp.uint32)
    hi_b = lax.bitcast_convert_type(hi.astype(jnp.bfloat16).astype(jnp.float32), jnp.uint32)
    return (hi_b & jnp.uint32(0xFFFF0000)) | (lo_b >> 16)


def _unpack_bf16_pair(w):
    lo = lax.bitcast_convert_type(w << 16, jnp.float32)
    hi = lax.bitcast_convert_type(w & jnp.uint32(0xFFFF0000), jnp.float32)
    return lo.astype(_MXU), hi.astype(_MXU)


def _sigmoid(x):
    return 1.0 / (1.0 + jnp.exp(-x))


def _log_sigmoid(x):
    return -(jnp.maximum(-x, 0.0) + jnp.log1p(jnp.exp(-jnp.abs(x))))


def _tile(n, cap):
    if n <= cap:
        return n
    t = cap - cap % LANES
    while n % t:
        t -= LANES
    return t


def _params(*sem):
    return pltpu.CompilerParams(dimension_semantics=sem, vmem_limit_bytes=VMEM_LIMIT)


def _mod_kernel(s_ref, w_ref, b_ref, o_ref):
    s = s_ref[...]
    s = s * _sigmoid(s)
    o_ref[...] = _dot(s.astype(_MXU), w_ref[...].astype(_MXU)) + b_ref[...]


def _modulation(cond_rows, w_mod, b_mod):
    depth, d, n6 = w_mod.shape
    rows = cond_rows.shape[0]
    tn = _tile(n6, 1024)
    return pl.pallas_call(
        _mod_kernel,
        out_shape=jax.ShapeDtypeStruct((depth, rows, n6), jnp.float32),
        grid=(depth, n6 // tn),
        in_specs=[
            pl.BlockSpec((rows, d), lambda l, j: (0, 0)),
            pl.BlockSpec((None, d, tn), lambda l, j: (l, 0, j)),
            pl.BlockSpec((None, 1, tn), lambda l, j: (l, 0, j)),
        ],
        out_specs=pl.BlockSpec((None, rows, tn), lambda l, j: (l, 0, j)),
        compiler_params=_params("arbitrary", "arbitrary"),
        name="modulation",
    )(cond_rows, w_mod, b_mod.reshape(depth, 1, n6))


def _rope(y, cos, sin_signed):
    lane = lax.broadcasted_iota(jnp.int32, y.shape, 1)
    first = (lane % (2 * (AXIS_DIM // 2))) < (AXIS_DIM // 2)
    partner = jnp.where(first, pltpu.roll(y, HEAD_DIM - AXIS_DIM // 2, 1), pltpu.roll(y, AXIS_DIM // 2, 1))
    return y * cos + partner * sin_signed


def _head_rms(y, g):
    return y * lax.rsqrt(jnp.mean(y * y, axis=-1, keepdims=True) + EPS) * g


def _proj_kernel(x_ref, g_ref, sh_ref, sc_ref, w_ref, cos_ref, sin_ref, qg_ref, kg_ref, *rest,
                 runs, tn, with_gates):
    if with_gates:
        wg_ref, bg_ref, o_ref, og_ref, h_sc, acc_sc = rest
    else:
        o_ref, h_sc, acc_sc = rest
    j = pl.program_id(2)
    n_tiles = runs[-1][1]

    @pl.when(j == 0)
    def _():
        x = x_ref[...]
        y = x * lax.rsqrt(jnp.mean(x * x, axis=-1, keepdims=True) + EPS) * g_ref[...]
        h = y * (1.0 + sc_ref[...]) + sh_ref[...]
        h_sc[...] = h.astype(_MXU)
        if with_gates:
            og_ref[...] = _dot(h_sc[...], wg_ref[...].astype(_MXU)) + bg_ref[...]
        acc_sc[...] = _dot(h_sc[...], w_ref[...].astype(_MXU))
        o_ref[...] = jnp.zeros(o_ref.shape, o_ref.dtype)

    def epilogue(ops, prev):
        for t, (norm, rope, scale) in enumerate(ops):
            y = prev[:, t * HEAD_DIM:(t + 1) * HEAD_DIM]
            if norm == "q":
                y = _head_rms(y, qg_ref[...])
            elif norm == "k":
                y = _head_rms(y, kg_ref[...])
            if rope:
                y = _rope(y, cos_ref[...], sin_ref[...])
            if scale != 1.0:
                y = y * scale
            o_ref[:, t * HEAD_DIM:(t + 1) * HEAD_DIM] = y

    for lo, hi, ops in runs:
        @pl.when((j - 1 >= lo) & (j - 1 < hi) & (j < n_tiles))
        def _(ops=ops):
            prev = acc_sc[...]
            nxt = _dot(h_sc[...], w_ref[...].astype(_MXU))
            epilogue(ops, prev)
            acc_sc[...] = nxt

    @pl.when(j == n_tiles)
    def _():
        epilogue(runs[-1][2], acc_sc[...])


def _tile_runs(head_ops, tn):
    per = tn // HEAD_DIM
    tiles = [tuple(head_ops[i * per:(i + 1) * per]) for i in range(len(head_ops) // per)]
    runs, lo = [], 0
    for i in range(1, len(tiles) + 1):
        if i == len(tiles) or tiles[i] != tiles[lo]:
            runs.append((lo, i, tiles[lo]))
            lo = i
    return tuple(runs)


def _project(x, mods, layer, norm_g, w_in, jl, n_main, head_ops, cos_t, sin_t, qg, kg, gates=None):
    G, T, D = x.shape
    tm = min(1024, T)
    tn = _tile(n_main, 512)
    runs = _tile_runs(head_ops, tn)
    n_tiles = n_main // tn
    with_gates = gates is not None
    in_specs = [
        pl.BlockSpec((None, tm, D), lambda b, i, j: (b, i, 0)),
        pl.BlockSpec((None, 1, D), lambda b, i, j: (layer, 0, 0)),
        pl.BlockSpec((None, None, None, 1, D), lambda b, i, j: (layer, 0, b, 0, 0)),
        pl.BlockSpec((None, None, None, 1, D), lambda b, i, j: (layer, 1, b, 0, 0)),
        pl.BlockSpec((None, D, tn), lambda b, i, j: (jl, 0, jnp.minimum(j, n_tiles - 1))),
        pl.BlockSpec((None, tm, HEAD_DIM), lambda b, i, j: (jnp.minimum(b, 1), i, 0)),
        pl.BlockSpec((None, tm, HEAD_DIM), lambda b, i, j: (jnp.minimum(b, 1), i, 0)),
        pl.BlockSpec((1, HEAD_DIM), lambda b, i, j: (0, 0)),
        pl.BlockSpec((1, HEAD_DIM), lambda b, i, j: (0, 0)),
    ]
    args = [x, norm_g.reshape(norm_g.shape[0], 1, D), mods, mods, w_in, cos_t, sin_t, qg, kg]
    out_shape = [jax.ShapeDtypeStruct((G, T, n_main), jnp.float32)]
    out_specs = [pl.BlockSpec((None, tm, tn), lambda b, i, j: (b, i, jnp.maximum(j - 1, 0)))]
    if with_gates:
        wg, bg = gates
        in_specs += [pl.BlockSpec((D, LANES), lambda b, i, j: (0, 0)),
                     pl.BlockSpec((1, LANES), lambda b, i, j: (0, 0))]
        args += [wg, bg]
        out_shape.append(jax.ShapeDtypeStruct((G, T, LANES), jnp.float32))
        out_specs.append(pl.BlockSpec((None, tm, LANES), lambda b, i, j: (b, i, 0)))
    res = pl.pallas_call(
        functools.partial(_proj_kernel, runs=runs, tn=tn, with_gates=with_gates),
        out_shape=out_shape,
        grid=(G, T // tm, n_tiles + 1),
        in_specs=in_specs,
        out_specs=out_specs,
        scratch_shapes=[pltpu.VMEM((tm, D), _MXU), pltpu.VMEM((tm, tn), jnp.float32)],
        compiler_params=_params("arbitrary", "arbitrary", "arbitrary"),
        name="norm_mod_project",
    )(*args)
    return res if with_gates else res[0]


def _flash_kernel(*refs, groups, tq, tk, n_lat, dv, has_cache, has_sink, diff, lam_init):
    it = iter(refs)
    q_ref, k_ref, v_ref = next(it), next(it), next(it)
    ck_refs = [next(it) for _ in groups] if has_cache else []
    cv_ref = next(it) if has_cache else None
    sink_ref = next(it) if has_sink else None
    if diff:
        lam_ref, subg_ref = next(it), next(it)
    o_ref = next(it)
    scr = [(next(it), next(it), next(it)) for _ in groups]
    kvh = pl.program_id(1)

    qs = []
    for gi, (q_offs, k_off) in enumerate(groups):
        q = jnp.concatenate([q_ref[:, o:o + HEAD_DIM] for o in q_offs], axis=0).astype(_MXU)
        qs.append(q)
        m_sc, l_sc, acc_sc = scr[gi]
        if has_sink:
            n_h = len(q_offs)
            m_sc[...] = jnp.concatenate(
                [jnp.full((tq, LANES), sink_ref[kvh * n_h + t] * LOG2E, jnp.float32) for t in range(n_h)], axis=0)
            lane = lax.broadcasted_iota(jnp.int32, l_sc.shape, 1)
            l_sc[...] = jnp.where(lane == 0, 1.0, 0.0)
        else:
            m_sc[...] = jnp.full(m_sc.shape, NEG_INF, jnp.float32)
            l_sc[...] = jnp.zeros(l_sc.shape, jnp.float32)
        acc_sc[...] = jnp.zeros(acc_sc.shape, jnp.float32)

    def update(gi, k, v):
        m_sc, l_sc, acc_sc = scr[gi]
        s = _dot_nt(qs[gi], k)
        tiles = [s[:, c * LANES:(c + 1) * LANES] for c in range(s.shape[1] // LANES)]
        mx = functools.reduce(jnp.maximum, tiles)
        m_prev = m_sc[...]
        m_new = jnp.maximum(m_prev, jnp.max(mx, axis=-1, keepdims=True))
        alpha = jnp.exp2(m_prev - m_new)
        ps = [jnp.exp2(t - m_new) for t in tiles]
        l_sc[...] = alpha * l_sc[...] + functools.reduce(jnp.add, ps)
        pv = _dot(jnp.concatenate(ps, axis=-1).astype(_MXU), v)
        for c in range(dv // LANES):
            cs = slice(c * LANES, (c + 1) * LANES)
            acc_sc[:, cs] = alpha * acc_sc[:, cs] + pv[:, cs]
        m_sc[...] = m_new

    def chunk(c, carry):
        r0 = pl.multiple_of(c * tk, tk)
        v = v_ref[pl.ds(r0, tk), :].astype(_MXU)
        for gi, (q_offs, k_off) in enumerate(groups):
            k = k_ref[pl.ds(r0, tk), k_off:k_off + HEAD_DIM].astype(_MXU)
            update(gi, k, v)
        return carry

    lax.fori_loop(0, n_lat // tk, chunk, 0)
    if has_cache:
        v = cv_ref[...].astype(_MXU)
        for gi in range(len(groups)):
            update(gi, ck_refs[gi][...].astype(_MXU), v)

    outs = []
    for gi in range(len(groups)):
        m_sc, l_sc, acc_sc = scr[gi]
        outs.append(acc_sc[...] / jnp.sum(l_sc[...], axis=-1, keepdims=True))
    if diff:
        l32 = lam_ref[...]
        lam = (jnp.exp(jnp.sum(l32[0:1] * l32[1:2], axis=-1, keepdims=True))
               - jnp.exp(jnp.sum(l32[2:3] * l32[3:4], axis=-1, keepdims=True)) + lam_init)
        ob = outs[0] - lam * outs[1]
        ob = ob * lax.rsqrt(jnp.mean(ob * ob, axis=-1, keepdims=True) + EPS) * subg_ref[...]
        o_ref[...] = (ob * (1.0 - lam_init)).astype(o_ref.dtype)
    else:
        o = outs[0]
        for t in range(len(groups[0][0])):
            o_ref[:, t * dv:(t + 1) * dv] = o[t * tq:(t + 1) * tq].astype(o_ref.dtype)


def _attention(p_arr, o_arr, *, n_seq, seq_len, b_off, n_kvh, groups, q_col, q_w, k_col, k_w, v_col, dv,
               o_col, o_w, tq, cache=None, sink=None, diff=None, name):
    G, T, _ = p_arr.shape
    per_b = T // seq_len
    nq = seq_len // tq
    tk = min(1024, seq_len)

    def row_map(s, i, blocks_per_seq):
        return b_off + s // per_b, (s % per_b) * blocks_per_seq + i

    def q_map(s, h, i):
        b, r = row_map(s, i, nq)
        return b, r, q_col // q_w + h

    def k_map(s, h, i):
        b, r = row_map(s, 0, 1)
        return b, r, k_col // k_w + h

    def v_map(s, h, i):
        b, r = row_map(s, 0, 1)
        return b, r, v_col // dv + h

    def o_map(s, h, i):
        b, r = row_map(s, i, nq)
        return b, r, o_col // o_w + h

    in_specs = [pl.BlockSpec((None, tq, q_w), q_map),
                pl.BlockSpec((None, seq_len, k_w), k_map),
                pl.BlockSpec((None, seq_len, dv), v_map)]
    args = [p_arr, p_arr, p_arr]
    if cache is not None:
        ck_list, cv = cache
        for arr, spec in ck_list:
            in_specs.append(spec)
            args.append(arr)
        in_specs.append(cv[1])
        args.append(cv[0])
    if sink is not None:
        in_specs.append(pl.BlockSpec(memory_space=pltpu.SMEM))
        args.append(sink)
    lam_init = 0.0
    if diff is not None:
        lam_arr, subg, lam_init = diff
        in_specs += [pl.BlockSpec((4, HEAD_DIM), lambda s, h, i: (0, 0)),
                     pl.BlockSpec((1, dv), lambda s, h, i: (0, 0))]
        args += [lam_arr, subg]
    n_in = len(args)
    in_specs.append(pl.BlockSpec((None, tq, o_w), o_map))
    args.append(o_arr)
    scratch = []
    for q_offs, _ in groups:
        m = len(q_offs) * tq
        scratch += [pltpu.VMEM((m, LANES), jnp.float32), pltpu.VMEM((m, LANES), jnp.float32),
                    pltpu.VMEM((m, dv), jnp.float32)]

    def body(*refs):
        refs = refs[:n_in] + refs[n_in + 1:]
        _flash_kernel(*refs, groups=groups, tq=tq, tk=tk, n_lat=seq_len, dv=dv,
                      has_cache=cache is not None, has_sink=sink is not None,
                      diff=diff is not None, lam_init=lam_init)

    return pl.pallas_call(
        body,
        out_shape=jax.ShapeDtypeStruct(o_arr.shape, o_arr.dtype),
        grid=(n_seq, n_kvh, nq),
        in_specs=in_specs,
        out_specs=pl.BlockSpec((None, tq, o_w), o_map),
        scratch_shapes=scratch,
        input_output_aliases={n_in: 0},
        compiler_params=_params("arbitrary", "arbitrary", "arbitrary"),
        name=name,
    )(*args)


def _window_kernel(q_ref, k_ref, v_ref, ck_ref, cv_ref, sink_ref, o_ref, *, tq, n_h, seq_len):
    kvh = pl.program_id(1)
    i = pl.program_id(2)
    span = tq + 2 * WINDOW
    start = jnp.clip(i * tq - WINDOW, 0, seq_len - span)
    start = pl.multiple_of(start, WINDOW)
    k = k_ref[pl.ds(start, span), :].astype(_MXU)
    v = v_ref[pl.ds(start, span), :].astype(_MXU)
    q = jnp.concatenate([q_ref[:, t * HEAD_DIM:(t + 1) * HEAD_DIM] for t in range(n_h)], axis=0).astype(_MXU)
    m_rows = n_h * tq
    s_loc = _dot_nt(q, k)
    row = lax.broadcasted_iota(jnp.int32, (m_rows, span), 0)
    col = lax.broadcasted_iota(jnp.int32, (m_rows, span), 1)
    qpos = i * tq + (row & (tq - 1))
    kpos = start + col
    s_loc = jnp.where(jnp.abs(qpos - kpos) <= WINDOW, s_loc, NEG_INF)
    s_ctx = _dot_nt(q, ck_ref[...].astype(_MXU))
    sk = jnp.concatenate(
        [jnp.full((tq, LANES), sink_ref[kvh * n_h + t] * LOG2E, jnp.float32) for t in range(n_h)], axis=0)
    loc_tiles = [s_loc[:, c * LANES:(c + 1) * LANES] for c in range(span // LANES)]
    ctx_tiles = [s_ctx[:, c * LANES:(c + 1) * LANES] for c in range(s_ctx.shape[1] // LANES)]
    mx = functools.reduce(jnp.maximum, loc_tiles + ctx_tiles)
    m = jnp.maximum(jnp.max(mx, axis=-1, keepdims=True), sk)
    p_loc = [jnp.exp2(t - m) for t in loc_tiles]
    p_ctx = [jnp.exp2(t - m) for t in ctx_tiles]
    den = (jnp.sum(functools.reduce(jnp.add, p_loc + p_ctx), axis=-1, keepdims=True)
           + jnp.exp2(sk - m)[:, 0:1])
    o = (_dot(jnp.concatenate(p_loc, axis=-1).astype(_MXU), v)
         + _dot(jnp.concatenate(p_ctx, axis=-1).astype(_MXU), cv_ref[...].astype(_MXU))) / den
    for t in range(n_h):
        o_ref[:, t * HEAD_DIM:(t + 1) * HEAD_DIM] = o[t * tq:(t + 1) * tq].astype(o_ref.dtype)


def _window_attention(p_arr, o_arr, cache_k, cache_v, jl, sink, *, b_off, n_b, q_col, k_col, v_col):
    G, T, _ = p_arr.shape
    n_h = C_HEADS // C_KV_HEADS
    q_w = n_h * HEAD_DIM
    tq = min(256, T - 2 * WINDOW)
    assert tq & (tq - 1) == 0 and T % tq == 0
    past = cache_k.shape[-2]
    in_specs = [
        pl.BlockSpec((None, tq, q_w), lambda b, h, i: (b + b_off, i, q_col // q_w + h)),
        pl.BlockSpec((None, T, HEAD_DIM), lambda b, h, i: (b + b_off, 0, k_col // HEAD_DIM + h)),
        pl.BlockSpec((None, T, HEAD_DIM), lambda b, h, i: (b + b_off, 0, v_col // HEAD_DIM + h)),
        pl.BlockSpec((None, None, None, past, HEAD_DIM), lambda b, h, i: (b, jl, h, 0, 0)),
        pl.BlockSpec((None, None, None, past, HEAD_DIM), lambda b, h, i: (b, jl, h, 0, 0)),
        pl.BlockSpec(memory_space=pltpu.SMEM),
        pl.BlockSpec((None, tq, q_w), lambda b, h, i: (b + b_off, i, h)),
    ]

    def body(q_ref, k_ref, v_ref, ck_ref, cv_ref, sink_ref, _o_in, o_ref):
        _window_kernel(q_ref, k_ref, v_ref, ck_ref, cv_ref, sink_ref, o_ref, tq=tq, n_h=n_h, seq_len=T)

    return pl.pallas_call(
        body,
        out_shape=jax.ShapeDtypeStruct(o_arr.shape, o_arr.dtype),
        grid=(n_b, C_KV_HEADS, T // tq),
        in_specs=in_specs,
        out_specs=pl.BlockSpec((None, tq, q_w), lambda b, h, i: (b + b_off, i, h)),
        input_output_aliases={6: 0},
        compiler_params=_params("arbitrary", "arbitrary", "arbitrary"),
        name="window_attention",
    )(p_arr, p_arr, p_arr, cache_k, cache_v, sink, o_arr)


def _mlstm_kernel(*refs, nc, has_init, emit_state):
    it = iter(refs)
    qf, kf, vf0, vf1, qb, kb, vb0, vb1 = [next(it) for _ in range(8)]
    gcf, grf, gcb, grb = [next(it) for _ in range(4)]
    if has_init:
        c0_ref, n0_ref, m0_ref = next(it), next(it), next(it)
    _hf_in, _hb_in = next(it), next(it)
    hf_ref, hb_ref = next(it), next(it)
    if emit_state:
        cout_ref, nout_ref, mout_ref = next(it), next(it), next(it)
    c_sc, n_sc, m_sc = next(it), next(it), next(it)
    j = pl.program_id(1)
    L = CHUNK
    nch = 2 * D_HEADS

    @pl.when(j == 0)
    def _():
        if has_init:
            c_sc[...] = c0_ref[...]
            n_sc[...] = n0_ref[...]
            m_sc[...] = m0_ref[...]
        else:
            c_sc[...] = jnp.zeros(c_sc.shape, jnp.float32)
            n_sc[...] = jnp.zeros(n_sc.shape, jnp.float32)
            m_sc[...] = jnp.zeros(m_sc.shape, jnp.float32)

    row = lax.broadcasted_iota(jnp.int32, (L, L), 0)
    col = lax.broadcasted_iota(jnp.int32, (L, L), 1)
    lower = col <= row
    upper = col >= row
    lower_m = jnp.where(lower, 1.0, 0.0).astype(_MXU)
    upper_m = jnp.where(upper, 1.0, 0.0).astype(_MXU)

    def exact_dot(a, b, a_is_data):
        if a_is_data:
            hi, lo = _split_hi_lo(a)
            return _dot(hi, b) + _dot(lo, b)
        hi, lo = _split_hi_lo(b)
        return _dot(a, hi) + _dot(a, lo)

    for d in range(2):
        rev = d == 1
        q_ref, k_ref, v_refs = (qb, kb, (vb0, vb1)) if rev else (qf, kf, (vf0, vf1))
        gc_ref, gr_ref = (gcb, grb) if rev else (gcf, grf)
        h_ref = hb_ref if rev else hf_ref
        gc = gc_ref[...]
        gr = gr_ref[...]
        lf_c = _log_sigmoid(gc)
        lf_r = _log_sigmoid(gr)
        b_cols = exact_dot(upper_m if rev else lower_m, lf_c, False)
        b_rows = exact_dot(lf_r, lower_m if rev else upper_m, True)
        tri = upper if rev else lower
        for hd in range(D_HEADS):
            ci = d * D_HEADS + hd
            ic, fc = d * 2 * D_HEADS + hd, d * 2 * D_HEADS + D_HEADS + hd
            ig_c, ig_r = gc[:, ic:ic + 1], gr[ic:ic + 1, :]
            b_c, b_r = b_cols[:, fc:fc + 1], b_rows[fc:fc + 1, :]
            q = q_ref[:, hd * D_QK:(hd + 1) * D_QK]
            k = k_ref[:, hd * D_QK:(hd + 1) * D_QK]
            v_ref = v_refs[hd // 2]
            v = v_ref[:, (hd % 2) * D_V:(hd % 2 + 1) * D_V]
            m = m_sc[ci:ci + 1, 0:1]
            n = n_sc[ci:ci + 1, :]
            c_mat = c_sc[ci]
            qm, km, vm = q.astype(_MXU), k.astype(_MXU), v.astype(_MXU)

            dmat = jnp.where(tri, b_c - b_r + ig_r, NEG_INF)
            m_prev = b_c + m
            m_t = jnp.maximum(m_prev, jnp.max(dmat, axis=-1, keepdims=True))
            w_intra = jnp.exp(dmat - m_t)
            w_prev = jnp.exp(m_prev - m_t)
            s = _dot_nt(qm, km) * w_intra
            num = w_prev * _dot(qm, c_mat.astype(_MXU)) + _dot(s.astype(_MXU), vm)
            den = w_prev * jnp.sum(q * n, axis=-1, keepdims=True) + jnp.sum(s, axis=-1, keepdims=True)
            h = num / jnp.maximum(jnp.abs(den), jnp.exp(-m_t))
            h_ref[:, hd * D_V:(hd + 1) * D_V] = h.astype(h_ref.dtype)

            b_end = b_c[0:1] if rev else b_c[L - 1:L]
            g_c = b_end - b_c + ig_c
            g_r = b_end - b_r + ig_r
            m_new = jnp.maximum(b_end + m, jnp.max(g_r, axis=-1, keepdims=True))
            wk = jnp.exp(g_c - m_new)
            decay = jnp.exp(b_end + m - m_new)
            c_sc[ci] = decay * c_mat + _dot_tn(km, (wk * v).astype(_MXU))
            n_sc[ci:ci + 1, :] = decay * n + jnp.sum(wk * k, axis=0, keepdims=True)
            m_sc[ci:ci + 1, :] = jnp.broadcast_to(m_new, (1, LANES))

    if emit_state:
        @pl.when(j == nc - 1)
        def _():
            cout_ref[...] = c_sc[...]
            nout_ref[...] = n_sc[...]
            mout_ref[...] = m_sc[...]


def _mlstm(p_arr, gates, gates_t, hf_arr, hb_arr, *, n_seq, seq_len, b_off, q_col, k_col, v_col,
           init=None, emit_state=False):
    G, T, _ = p_arr.shape
    per_b = T // seq_len
    nc = seq_len // CHUNK
    wq = D_HEADS * D_QK
    hw = D_HEADS * D_V
    nch = 2 * D_HEADS

    def rows(s, j, rev):
        jj = nc - 1 - j if rev else j
        return b_off + s // per_b, (s % per_b) * nc + jj

    def col_spec(width, col, rev):
        return pl.BlockSpec((None, CHUNK, width), lambda s, j: rows(s, j, rev) + (col // width,))

    def gate_t_spec(rev):
        def imap(s, j):
            b, r = rows(s, j, rev)
            return b, 0, r
        return pl.BlockSpec((None, N_GATES, CHUNK), imap)

    in_specs, args = [], []
    for rev in (False, True):
        in_specs += [col_spec(wq, q_col, rev), col_spec(wq, k_col, rev),
                     col_spec(wq, v_col, rev), col_spec(wq, v_col + wq, rev)]
        args += [p_arr] * 4
    for rev in (False, True):
        in_specs += [col_spec(LANES, 0, rev), gate_t_spec(rev)]
        args += [gates, gates_t]
    if init is not None:
        c0, n0, m0, jl = init
        in_specs += [pl.BlockSpec((None, None, nch, D_QK, D_V), lambda s, j: (s, jl, 0, 0, 0)),
                     pl.BlockSpec((None, None, nch, D_QK), lambda s, j: (s, jl, 0, 0)),
                     pl.BlockSpec((None, None, nch, LANES), lambda s, j: (s, jl, 0, 0))]
        args += [c0, n0, m0]
    n_in = len(args)
    h_specs = [col_spec(hw, 0, False), col_spec(hw, 0, True)]
    in_specs += h_specs
    args += [hf_arr, hb_arr]
    out_shape = [jax.ShapeDtypeStruct(hf_arr.shape, hf_arr.dtype),
                 jax.ShapeDtypeStruct(hb_arr.shape, hb_arr.dtype)]
    out_specs = list(h_specs)
    if emit_state:
        out_shape += [jax.ShapeDtypeStruct((n_seq, nch, D_QK, D_V), jnp.float32),
                      jax.ShapeDtypeStruct((n_seq, nch, D_QK), jnp.float32),
                      jax.ShapeDtypeStruct((n_seq, nch, LANES), jnp.float32)]
        out_specs += [pl.BlockSpec((None, nch, D_QK, D_V), lambda s, j: (s, 0, 0, 0)),
                      pl.BlockSpec((None, nch, D_QK), lambda s, j: (s, 0, 0)),
                      pl.BlockSpec((None, nch, LANES), lambda s, j: (s, 0, 0))]
    return pl.pallas_call(
        functools.partial(_mlstm_kernel, nc=nc, has_init=init is not None, emit_state=emit_state),
        out_shape=out_shape,
        grid=(n_seq, nc),
        in_specs=in_specs,
        out_specs=out_specs,
        scratch_shapes=[pltpu.VMEM((nch, D_QK, D_V), jnp.float32), pltpu.VMEM((nch, D_QK), jnp.float32),
                        pltpu.VMEM((nch, LANES), jnp.float32)],
        input_output_aliases={n_in: 0, n_in + 1: 1},
        compiler_params=_params("arbitrary", "arbitrary"),
        name="mlstm",
    )(*args)


def _outproj_kernel(*refs, odd, nj, tn, n_experts):
    it = iter(refs)
    if odd:
        oc_ref, hf_ref, hb_ref, do0_ref, do1_ref, dng_ref = [next(it) for _ in range(6)]
    else:
        o_ref = next(it)
    x_ref, g1_ref, w_ref, ng_ref, sh2_ref, sc2_ref, rwh_ref, rwl_ref, rb_ref = [next(it) for _ in range(9)]
    xn_ref, f_ref, ti_ref, tg_ref, cnt_ref = [next(it) for _ in range(5)]
    xn_sc, cnt_sc = next(it), next(it)
    om_sc = next(it) if odd else None
    j = pl.program_id(2)
    first_tile = (pl.program_id(0) == 0) & (pl.program_id(1) == 0)

    if odd:
        @pl.when(j == 0)
        def _():
            wc = oc_ref.shape[1]
            om_sc[:, :wc] = oc_ref[...].astype(_MXU)
            hd = hf_ref[...] + hb_ref[...]
            half = do0_ref.shape[1]
            for hh in range(D_HEADS):
                y = hd[:, hh * D_V:(hh + 1) * D_V]
                y = y * lax.rsqrt(jnp.mean(y * y, axis=-1, keepdims=True) + EPS) * dng_ref[:, hh * D_V:(hh + 1) * D_V]
                do_ref = do0_ref if hh * D_V < half else do1_ref
                c0 = (hh * D_V) % half
                y = y * _sigmoid(do_ref[:, c0:c0 + D_V])
                om_sc[:, wc + hh * D_V:wc + (hh + 1) * D_V] = y.astype(_MXU)
        o = om_sc[...]
    else:
        o = o_ref[...].astype(_MXU)

    xn = x_ref[...] + g1_ref[...] * _dot(o, w_ref[...].astype(_MXU))
    xn_ref[...] = xn
    xn_sc[j] = xn

    @pl.when(j == nj - 1)
    def _():
        ss = jnp.zeros((xn.shape[0], 1), jnp.float32)
        for c in range(nj):
            t = xn_sc[c]
            ss = ss + jnp.sum(t * t, axis=-1, keepdims=True)
        inv = lax.rsqrt(ss / (nj * tn) + EPS)
        logits = rb_ref[...]
        for c in range(nj):
            cs = slice(c * tn, (c + 1) * tn)
            f = (xn_sc[c] * inv * ng_ref[:, cs]) * (1.0 + sc2_ref[:, cs]) + sh2_ref[:, cs]
            xn_sc[c] = f
            fh, fl = _split_hi_lo(f)
            logits = logits + (_dot(fh, rwh_ref[cs, :]) + _dot(fl, rwh_ref[cs, :]) + _dot(fh, rwl_ref[cs, :]))
        half = nj * tn // 2
        pw = min(tn, half)
        for pc in range(half // pw):
            lo_c, hi_c = pc * pw, half + pc * pw
            lo = xn_sc[lo_c // tn][:, lo_c % tn:lo_c % tn + pw]
            hi = xn_sc[hi_c // tn][:, hi_c % tn:hi_c % tn + pw]
            f_ref[:, pc * pw:(pc + 1) * pw] = _pack_bf16_pair(lo, hi)
        lane = lax.broadcasted_iota(jnp.int32, logits.shape, 1)
        lane_f = lane.astype(jnp.float32)
        cur = logits
        vals, idxs = [], []
        for _ in range(TOP_K):
            mx = jnp.max(cur, axis=-1, keepdims=True)
            ix = jnp.min(jnp.where(cur == mx, lane_f, float(LANES)), axis=-1, keepdims=True).astype(jnp.int32)
            vals.append(mx)
            idxs.append(ix)
            cur = jnp.where(lane == ix, -jnp.inf, cur)
        es = [jnp.exp(v - vals[0]) for v in vals]
        tot = es[0] + es[1] + es[2] + es[3]
        ti = jnp.zeros(logits.shape, jnp.int32)
        tg = jnp.zeros(logits.shape, jnp.float32)
        for kk in range(TOP_K):
            ti = jnp.where(lane == kk, idxs[kk], ti)
            tg = jnp.where(lane == kk, es[kk] / tot, tg)
        tg_ref[...] = tg

        @pl.when(first_tile)
        def _():
            cnt_sc[...] = jnp.zeros(cnt_sc.shape, jnp.float32)

        tm = logits.shape[0]
        onehot = [jnp.where(lane == idxs[kk], 1.0, 0.0) for kk in range(TOP_K)]
        picked = functools.reduce(jnp.add, onehot)
        r_i = lax.broadcasted_iota(jnp.int32, (tm, tm), 0)
        c_i = lax.broadcasted_iota(jnp.int32, (tm, tm), 1)
        earlier = jnp.where(c_i < r_i, 1.0, 0.0).astype(_MXU)
        before = cnt_sc[...] + _dot(earlier, picked.astype(_MXU))
        for kk in range(TOP_K):
            rank = jnp.sum(onehot[kk] * before, axis=-1, keepdims=True)
            ti = jnp.where(lane == TOP_K + kk, rank.astype(jnp.int32), ti)
        ti_ref[...] = ti
        cnt_sc[...] += jnp.sum(picked, axis=0, keepdims=True)
        cnt_ref[...] = cnt_sc[...]


def _out_project(x, mods, layer, w_out, norm_ffn_g, rwh, rwl, rb, n_experts, *, o=None, odd_in=None):
    G, T, D = x.shape
    mix = w_out.shape[1]
    tm = min(512, T)
    tn = _tile(D, 512)
    nj = D // tn
    odd = odd_in is not None
    bmap = lambda b, i, j: (b, i, 0)
    if odd:
        oc, hf, hb, p_cd, do_col, d_norm_g = odd_in
        half = (D_HEADS * D_V) // 2
        in_specs = [pl.BlockSpec((None, tm, oc.shape[-1]), bmap),
                    pl.BlockSpec((None, tm, hf.shape[-1]), bmap),
                    pl.BlockSpec((None, tm, hb.shape[-1]), bmap),
                    pl.BlockSpec((None, tm, half), lambda b, i, j: (b, i, do_col // half)),
                    pl.BlockSpec((None, tm, half), lambda b, i, j: (b, i, do_col // half + 1)),
                    pl.BlockSpec((1, D_HEADS * D_V), lambda b, i, j: (0, 0))]
        args = [oc, hf, hb, p_cd, p_cd, d_norm_g]
    else:
        in_specs = [pl.BlockSpec((None, tm, mix), bmap)]
        args = [o]
    in_specs += [
        pl.BlockSpec((None, tm, tn), lambda b, i, j: (b, i, j)),
        pl.BlockSpec((None, None, None, 1, tn), lambda b, i, j: (layer, 2, b, 0, j)),
        pl.BlockSpec((None, mix, tn), lambda b, i, j: (layer, 0, j)),
        pl.BlockSpec((None, 1, D), lambda b, i, j: (layer, 0, 0)),
        pl.BlockSpec((None, None, None, 1, D), lambda b, i, j: (layer, 3, b, 0, 0)),
        pl.BlockSpec((None, None, None, 1, D), lambda b, i, j: (layer, 4, b, 0, 0)),
        pl.BlockSpec((None, D, LANES), lambda b, i, j: (layer, 0, 0)),
        pl.BlockSpec((None, D, LANES), lambda b, i, j: (layer, 0, 0)),
        pl.BlockSpec((None, 1, LANES), lambda b, i, j: (layer, 0, 0)),
    ]
    args += [x, mods, w_out, norm_ffn_g.reshape(norm_ffn_g.shape[0], 1, D), mods, mods, rwh, rwl, rb]
    out_shape = [jax.ShapeDtypeStruct((G, T, D), jnp.float32),
                 jax.ShapeDtypeStruct((G, T, D // 2), jnp.uint32),
                 jax.ShapeDtypeStruct((G, T, LANES), jnp.int32),
                 jax.ShapeDtypeStruct((G, T, LANES), jnp.float32),
                 jax.ShapeDtypeStruct((1, LANES), jnp.float32)]
    out_specs = [pl.BlockSpec((None, tm, tn), lambda b, i, j: (b, i, j)),
                 pl.BlockSpec((None, tm, D // 2), bmap),
                 pl.BlockSpec((None, tm, LANES), bmap),
                 pl.BlockSpec((None, tm, LANES), bmap),
                 pl.BlockSpec((1, LANES), lambda b, i, j: (0, 0))]
    scratch = [pltpu.VMEM((nj, tm, tn), jnp.float32), pltpu.VMEM((1, LANES), jnp.float32)]
    if odd:
        scratch.append(pltpu.VMEM((tm, mix), _MXU))
    return pl.pallas_call(
        functools.partial(_outproj_kernel, odd=odd, nj=nj, tn=tn, n_experts=n_experts),
        out_shape=out_shape,
        grid=(G, T // tm, nj),
        in_specs=in_specs,
        out_specs=out_specs,
        scratch_shapes=scratch,
        compiler_params=_params("arbitrary", "arbitrary", "arbitrary"),
        name="out_project_router",
    )(*args)


def _moe_kernel(te_ref, ns_ref, x_ref, wg_ref, wu_ref, wd_ref, bg_ref, bu_ref, bd_ref, o_ref, *, n_sub):
    t = pl.program_id(0)
    c = pl.program_id(1)
    nsub = ns_ref[t]
    half = x_ref.shape[1]

    def expert_rows(row_slices):
        wg_lo, wg_hi = wg_ref[:half, :].astype(_MXU), wg_ref[half:, :].astype(_MXU)
        wu_lo, wu_hi = wu_ref[:half, :].astype(_MXU), wu_ref[half:, :].astype(_MXU)
        wd = wd_ref[...].astype(_MXU)
        parts = []
        for rs in row_slices:
            lo, hi = _unpack_bf16_pair(x_ref[rs, :])
            gt = (_dot(lo, wg_lo) + _dot(hi, wg_hi)) + bg_ref[...]
            up = (_dot(lo, wu_lo) + _dot(hi, wu_hi)) + bu_ref[...]
            gt = jnp.minimum(gt, SWIGLU_LIMIT)
            up = jnp.clip(up, -SWIGLU_LIMIT, SWIGLU_LIMIT)
            act = gt * _sigmoid(SWIGLU_ALPHA * gt) * (up + 1.0)
            parts.append(_dot(act.astype(_MXU), wd))

        @pl.when(c == 0)
        def _():
            for rs, part in zip(row_slices, parts):
                o_ref[rs, :] = part + bd_ref[...]

        @pl.when(c > 0)
        def _():
            for rs, part in zip(row_slices, parts):
                o_ref[rs, :] += part

    @pl.when(nsub == n_sub)
    def _():
        expert_rows([slice(0, n_sub * MOE_SUB)])

    for s in range(n_sub):
        rs = slice(s * MOE_SUB, (s + 1) * MOE_SUB)

        @pl.when((s < nsub) & (nsub < n_sub))
        def _(rs=rs):
            expert_rows([rs])

        @pl.when((s >= nsub) & (c == 0))
        def _(rs=rs):
            o_ref[rs, :] = jnp.zeros((MOE_SUB, o_ref.shape[1]), jnp.float32)


def _moe_experts(xb, tile_expert, tile_nsub, w_gate_up, b_gate_up, w_down, b_down, layer):
    P, half = xb.shape
    E, D, F2 = w_gate_up.shape[1:]
    F = F2 // 2
    fc = _tile(F, 512)
    nfc = F // fc
    n_tiles = P // MOE_TILE
    grid_spec = pltpu.PrefetchScalarGridSpec(
        num_scalar_prefetch=2,
        grid=(n_tiles, nfc),
        in_specs=[
            pl.BlockSpec((MOE_TILE, half), lambda t, c, te, ns: (t, 0)),
            pl.BlockSpec((None, None, D, fc), lambda t, c, te, ns: (layer, te[t], 0, c)),
            pl.BlockSpec((None, None, D, fc), lambda t, c, te, ns: (layer, te[t], 0, nfc + c)),
            pl.BlockSpec((None, None, fc, D), lambda t, c, te, ns: (layer, te[t], c, 0)),
            pl.BlockSpec((None, None, 1, fc), lambda t, c, te, ns: (layer, te[t], 0, c)),
            pl.BlockSpec((None, None, 1, fc), lambda t, c, te, ns: (layer, te[t], 0, nfc + c)),
            pl.BlockSpec((None, None, 1, D), lambda t, c, te, ns: (layer, te[t], 0, 0)),
        ],
        out_specs=pl.BlockSpec((MOE_TILE, D), lambda t, c, te, ns: (t, 0)),
    )
    depth = w_gate_up.shape[0]
    return pl.pallas_call(
        functools.partial(_moe_kernel, n_sub=MOE_TILE // MOE_SUB),
        out_shape=jax.ShapeDtypeStruct((P, D), jnp.float32),
        grid_spec=grid_spec,
        compiler_params=_params("arbitrary", "arbitrary"),
        name="moe_experts",
    )(tile_expert, tile_nsub, xb, w_gate_up, w_gate_up, w_down,
      b_gate_up.reshape(depth, E, 1, F2), b_gate_up.reshape(depth, E, 1, F2), b_down.reshape(depth, E, 1, D))


def _dispatch_kernel(pos_ref, f_ref, _xb_in, xb_ref, sem, *, tt):
    def issue(r, carry):
        for k in range(TOP_K):
            dst = pos_ref[0, r * TOP_K + k]
            pltpu.make_async_copy(f_ref.at[pl.ds(r, 1)], xb_ref.at[pl.ds(dst, 1)], sem).start()
        return carry

    lax.fori_loop(0, tt, issue, 0)
    for k in range(TOP_K):
        pltpu.make_async_copy(f_ref, f_ref, sem).wait()


def _dispatch(f_packed, pos, n_rows):
    N, half = f_packed.shape
    tt = min(512, N)
    xb0 = jnp.zeros((n_rows, half), f_packed.dtype)
    return pl.pallas_call(
        functools.partial(_dispatch_kernel, tt=tt),
        out_shape=jax.ShapeDtypeStruct((n_rows, half), f_packed.dtype),
        grid=(N // tt,),
        in_specs=[pl.BlockSpec((None, 1, tt * TOP_K), lambda i: (i, 0, 0), memory_space=pltpu.SMEM),
                  pl.BlockSpec((tt, half), lambda i: (i, 0)),
                  pl.BlockSpec(memory_space=pl.ANY)],
        out_specs=pl.BlockSpec(memory_space=pl.ANY),
        scratch_shapes=[pltpu.SemaphoreType.DMA],
        input_output_aliases={2: 0},
        compiler_params=_params("arbitrary"),
        name="moe_dispatch",
    )(pos.reshape(N // tt, 1, tt * TOP_K), f_packed, xb0)


def _combine_kernel(pos_ref, posn_ref, gate_ref, x_ref, g2_ref, *rest, tt, n_steps, final):
    if final:
        ng_ref, yb_ref, o_ref, buf, sem = rest
    else:
        yb_ref, o_ref, buf, sem = rest
    i = pl.program_id(0)
    slot = i % 2

    def gather(p_ref, sl):
        def issue(r, carry):
            for k in range(TOP_K):
                src = p_ref[0, r * TOP_K + k]
                pltpu.make_async_copy(yb_ref.at[pl.ds(src, 1)], buf.at[sl, k, pl.ds(r, 1)], sem.at[sl]).start()
            return carry
        lax.fori_loop(0, tt, issue, 0)

    @pl.when(i == 0)
    def _():
        gather(pos_ref, 0)

    @pl.when(i + 1 < n_steps)
    def _():
        gather(posn_ref, 1 - slot)

    for k in range(TOP_K):
        pltpu.make_async_copy(buf.at[slot, k], buf.at[slot, k], sem.at[slot]).wait()
    gate = gate_ref[...]
    y = gate[:, 0:1] * buf[slot, 0]
    for k in range(1, TOP_K):
        y = y + gate[:, k:k + 1] * buf[slot, k]
    xn = x_ref[...] + g2_ref[...] * y
    if final:
        xn = xn * lax.rsqrt(jnp.mean(xn * xn, axis=-1, keepdims=True) + EPS) * ng_ref[...]
    o_ref[...] = xn


def _combine(x, yb, pos, top_gate, mods, layer, final_g):
    G, T, D = x.shape
    N = G * T
    tt = min(256, T)
    per_b = T // tt
    n_steps = N // tt
    final = final_g is not None
    pos3 = pos.reshape(n_steps, 1, tt * TOP_K)
    in_specs = [
        pl.BlockSpec((None, 1, tt * TOP_K), lambda i: (i, 0, 0), memory_space=pltpu.SMEM),
        pl.BlockSpec((None, 1, tt * TOP_K), lambda i: (jnp.minimum(i + 1, n_steps - 1), 0, 0),
                     memory_space=pltpu.SMEM),
        pl.BlockSpec((tt, LANES), lambda i: (i, 0)),
        pl.BlockSpec((tt, D), lambda i: (i, 0)),
        pl.BlockSpec((None, None, None, 1, D), lambda i: (layer, 5, i // per_b, 0, 0)),
    ]
    args = [pos3, pos3, top_gate, x.reshape(N, D), mods]
    if final:
        in_specs.append(pl.BlockSpec((1, D), lambda i: (0, 0)))
        args.append(final_g.reshape(1, D))
    in_specs.append(pl.BlockSpec(memory_space=pl.ANY))
    args.append(yb)
    out = pl.pallas_call(
        functools.partial(_combine_kernel, tt=tt, n_steps=n_steps, final=final),
        out_shape=jax.ShapeDtypeStruct((N, D), jnp.float32),
        grid=(n_steps,),
        in_specs=in_specs,
        out_specs=pl.BlockSpec((tt, D), lambda i: (i, 0)),
        scratch_shapes=[pltpu.VMEM((2, TOP_K, tt, D), jnp.float32), pltpu.SemaphoreType.DMA((2,))],
        compiler_params=_params("arbitrary"),
        name="moe_combine",
    )(*args)
    return out.reshape(G, T, D)


def _moe_layer(x, f_packed, top_i, top_g, cnt, mods, layer, w_gate_up, b_gate_up, w_down, b_down, final_g):
    G, T, D = x.shape
    N = G * T
    E = w_gate_up.shape[1]
    counts = cnt[0, :E].astype(jnp.int32)
    padded = (counts + MOE_TILE - 1) // MOE_TILE * MOE_TILE
    pad_end = jnp.cumsum(padded)
    pad_start = pad_end - padded
    ti = top_i.reshape(N, LANES)
    top_e, rank = ti[:, :TOP_K], ti[:, TOP_K:2 * TOP_K]
    start_of = jnp.sum(jnp.where(top_e[:, :, None] == jnp.arange(E, dtype=jnp.int32), pad_start, 0), axis=-1)
    pos = (start_of + rank).astype(jnp.int32)
    n_tiles = (N * TOP_K + MOE_TILE - 1) // MOE_TILE + E
    tile_row0 = jnp.arange(n_tiles, dtype=jnp.int32) * MOE_TILE
    tile_expert = jnp.minimum(jnp.searchsorted(pad_end, tile_row0, side="right"), E - 1).astype(jnp.int32)
    valid = jnp.clip(counts[tile_expert] - (tile_row0 - pad_start[tile_expert]), 0, MOE_TILE)
    valid = jnp.where(tile_row0 < pad_end[-1], valid, 0)
    tile_nsub = ((valid + MOE_SUB - 1) // MOE_SUB).astype(jnp.int32)
    xb = _dispatch(f_packed.reshape(N, D // 2), pos, n_tiles * MOE_TILE)
    yb = _moe_experts(xb, tile_expert, tile_nsub, w_gate_up, b_gate_up, w_down, b_down, layer)
    return _combine(x, yb, pos, top_g.reshape(N, LANES), mods, layer, final_g)


def _rope_tables(T):
    n_rows = T // GRID_W
    row = jnp.repeat(jnp.arange(n_rows, dtype=jnp.float32), GRID_W)
    col = jnp.tile(jnp.arange(GRID_W, dtype=jnp.float32), n_rows)
    inv = ROPE_THETA ** (-jnp.arange(0, AXIS_DIM, 2, dtype=jnp.float32) / AXIS_DIM)
    cr, sr = jnp.cos(row[:, None] * inv), jnp.sin(row[:, None] * inv)
    cc, sc = jnp.cos(col[:, None] * inv), jnp.sin(col[:, None] * inv)
    cos_t = jnp.concatenate([cr, cr, cc, cc], axis=-1)
    sin_t = jnp.concatenate([-sr, sr, -sc, sc], axis=-1)
    return (jnp.stack([jnp.ones_like(cos_t), cos_t]), jnp.stack([jnp.zeros_like(sin_t), sin_t]))


def _heads_out(p0, col, n, d, Bp, Tp):
    return p0[:, col:col + n * d].reshape(Bp, Tp, n, d).transpose(0, 2, 1, 3)


def kernel(x_prompt, x_sample, cache_a_k, cache_a_v, cache_b_k, cache_b_v, cache_c_k, cache_c_v, state_d_C, state_d_n, state_d_m, c, c_ctx, norm_mix_g, norm_ffn_g, w_mod, b_mod, w_out, w_in_ab, a_q_norm_g, a_k_norm_g, b_lambda, b_subln_g, w_in_cd, b_gates, c_sink, d_norm_g, router_w, router_b, w_gate_up, b_gate_up, w_down, b_down, final_norm_g):
    Bp, Tp, D = x_prompt.shape
    Bs, T, _ = x_sample.shape
    assert Bp * Tp == T, "context tokens are laid out as one extra batch of DEC_SEQ tokens"
    depth = w_mod.shape[0]
    E = router_w.shape[-1]
    past = cache_a_k.shape[-2]
    G = 1 + Bs
    scale = HEAD_DIM ** -0.5 * LOG2E
    f32 = jnp.float32

    x = jnp.concatenate([x_prompt.reshape(1, T, D), x_sample], axis=0)
    cond = jnp.concatenate([c_ctx[None], c, jnp.zeros((8 - G, D), f32)], axis=0)
    mods = _modulation(cond, w_mod, b_mod)
    mods = mods[:, :G].reshape(depth, G, 6, 1, D).transpose(0, 2, 1, 3, 4)

    cos_t, sin_t = _rope_tables(T)
    pad_e = LANES - E
    rw = jnp.pad(router_w, ((0, 0), (0, 0), (0, pad_e)))
    rwh = rw.astype(_MXU)
    rwl = (rw - rwh.astype(f32)).astype(_MXU)
    rb = jnp.pad(router_b, ((0, 0), (0, pad_e)), constant_values=NEG_INF).reshape(depth, 1, LANES)

    ab_ops = ([("q", True, scale)] * A_HEADS + [("k", True, 1.0)] * A_KV_HEADS + [(None, False, 1.0)] * A_KV_HEADS
              + [(None, True, scale)] * (2 * B_HEADS) + [(None, True, 1.0)] * (2 * B_HEADS)
              + [(None, False, 1.0)] * (B_HEADS * B_VDIM // HEAD_DIM))
    cd_ops = ([(None, True, scale)] * C_HEADS + [(None, True, 1.0)] * C_KV_HEADS + [(None, False, 1.0)] * C_KV_HEADS
              + [(None, False, 1.0)] * D_HEADS + [(None, False, D_QK ** -0.5)] * D_HEADS
              + [(None, False, 1.0)] * (2 * D_HEADS * D_V // HEAD_DIM))
    A_Q, A_K, A_V = 0, A_HEADS * HEAD_DIM, (A_HEADS + A_KV_HEADS) * HEAD_DIM
    B_Q = (A_HEADS + 2 * A_KV_HEADS) * HEAD_DIM
    B_K = B_Q + 2 * B_HEADS * HEAD_DIM
    B_V = B_K + 2 * B_HEADS * HEAD_DIM
    C_Q, C_K, C_V = 0, C_HEADS * HEAD_DIM, (C_HEADS + C_KV_HEADS) * HEAD_DIM
    D_Q = (C_HEADS + 2 * C_KV_HEADS) * HEAD_DIM
    D_K = D_Q + D_HEADS * D_QK
    D_VC = D_K + D_HEADS * D_QK
    D_O = D_VC + D_HEADS * D_V
    a_grp = A_HEADS // A_KV_HEADS
    c_grp = C_HEADS // C_KV_HEADS
    ones_g = jnp.ones((1, HEAD_DIM), f32)

    new = {k: [] for k in ("ak", "av", "bk", "bv", "ck", "cv", "dC", "dn", "dm")}

    for layer in range(depth):
        jl = layer // 2
        if layer % 2 == 0:
            lam_init = 0.8 - 0.6 * math.exp(-0.3 * layer)
            p = _project(x, mods, layer, norm_mix_g, w_in_ab, jl, AB_WIDTH, ab_ops, cos_t, sin_t,
                         a_q_norm_g[jl].reshape(1, HEAD_DIM), a_k_norm_g[jl].reshape(1, HEAD_DIM))
            p0 = p[0]
            new["ak"].append(_heads_out(p0, A_K, A_KV_HEADS, HEAD_DIM, Bp, Tp))
            new["av"].append(_heads_out(p0, A_V, A_KV_HEADS, HEAD_DIM, Bp, Tp))
            new["bk"].append(p0[:, B_K:B_V].reshape(Bp, Tp, B_HEADS, 2, HEAD_DIM).transpose(0, 3, 2, 1, 4))
            new["bv"].append(_heads_out(p0, B_V, B_HEADS, B_VDIM, Bp, Tp))

            o = jnp.zeros((G, T, MIX_WIDTH), _MXU)
            a_groups = ((tuple(t * HEAD_DIM for t in range(a_grp)), 0),)
            b_groups = (((0,), 0), ((HEAD_DIM,), HEAD_DIM))
            diff = (b_lambda[jl], b_subln_g[jl].reshape(1, B_VDIM), lam_init)
            a_common = dict(n_kvh=A_KV_HEADS, groups=a_groups, q_col=A_Q, q_w=a_grp * HEAD_DIM, k_col=A_K,
                            k_w=HEAD_DIM, v_col=A_V, dv=HEAD_DIM, o_col=0, o_w=a_grp * HEAD_DIM)
            b_common = dict(n_kvh=B_HEADS, groups=b_groups, q_col=B_Q, q_w=2 * HEAD_DIM, k_col=B_K,
                            k_w=2 * HEAD_DIM, v_col=B_V, dv=B_VDIM, o_col=A_HEADS * HEAD_DIM, o_w=B_VDIM, diff=diff)
            o = _attention(p, o, n_seq=Bp, seq_len=Tp, b_off=0, tq=min(256, Tp), name="ctx_attn_a", **a_common)
            o = _attention(p, o, n_seq=Bp, seq_len=Tp, b_off=0, tq=min(256, Tp), name="ctx_attn_b", **b_common)
            cache_a = ([(cache_a_k, pl.BlockSpec((None, None, None, past, HEAD_DIM),
                                                  lambda s, h, i: (s, jl, h, 0, 0)))],
                       (cache_a_v, pl.BlockSpec((None, None, None, past, HEAD_DIM),
                                                lambda s, h, i: (s, jl, h, 0, 0))))
            cache_b = ([(cache_b_k, pl.BlockSpec((None, None, None, None, past, HEAD_DIM),
                                                  lambda s, h, i, mm=mm: (s, jl, mm, h, 0, 0))) for mm in range(2)],
                       (cache_b_v, pl.BlockSpec((None, None, None, past, B_VDIM),
                                                lambda s, h, i: (s, jl, h, 0, 0))))
            o = _attention(p, o, n_seq=Bs, seq_len=T, b_off=1, tq=min(512, T), cache=cache_a,
                           name="latent_attn_a", **a_common)
            o = _attention(p, o, n_seq=Bs, seq_len=T, b_off=1, tq=min(1024, T), cache=cache_b,
                           name="latent_attn_b", **b_common)
            x, f_tok, top_i, top_g, cnt = _out_project(x, mods, layer, w_out, norm_ffn_g, rwh, rwl, rb, E, o=o)
        else:
            wg = jnp.pad(w_in_cd[jl][:, CD_MAIN:], ((0, 0), (0, LANES - N_GATES)))
            bg = jnp.pad(b_gates[jl], (0, LANES - N_GATES)).reshape(1, LANES)
            p, gates = _project(x, mods, layer, norm_mix_g, w_in_cd, jl, CD_MAIN, cd_ops, cos_t, sin_t,
                                ones_g, ones_g, gates=(wg, bg))
            gates_t = jnp.swapaxes(gates[:, :, :N_GATES], 1, 2)
            p0 = p[0]
            new["ck"].append(_heads_out(p0, C_K, C_KV_HEADS, HEAD_DIM, Bp, Tp))
            new["cv"].append(_heads_out(p0, C_V, C_KV_HEADS, HEAD_DIM, Bp, Tp))

            oc = jnp.zeros((G, T, C_HEADS * HEAD_DIM), _MXU)
            c_groups = ((tuple(t * HEAD_DIM for t in range(c_grp)), 0),)
            oc = _attention(p, oc, n_seq=Bp, seq_len=Tp, b_off=0, tq=min(256, Tp), n_kvh=C_KV_HEADS,
                            groups=c_groups, q_col=C_Q, q_w=c_grp * HEAD_DIM, k_col=C_K, k_w=HEAD_DIM,
                            v_col=C_V, dv=HEAD_DIM, o_col=0, o_w=c_grp * HEAD_DIM, sink=c_sink[jl],
                            name="ctx_attn_c")
            oc = _window_attention(p, oc, cache_c_k, cache_c_v, jl, c_sink[jl], b_off=1, n_b=Bs,
                                   q_col=C_Q, k_col=C_K, v_col=C_V)
            hf = jnp.zeros((G, T, D_HEADS * D_V), f32)
            hb = jnp.zeros((G, T, D_HEADS * D_V), f32)
            hf, hb, dC, dn, dm = _mlstm(p, gates, gates_t, hf, hb, n_seq=Bp, seq_len=Tp, b_off=0,
                                        q_col=D_Q, k_col=D_K, v_col=D_VC, emit_state=True)
            new["dC"].append(dC.reshape(Bp, 2, D_HEADS, D_QK, D_V))
            new["dn"].append(dn.reshape(Bp, 2, D_HEADS, D_QK))
            new["dm"].append(dm[:, :, 0].reshape(Bp, 2, D_HEADS))
            n_odd = state_d_C.shape[1]
            init = (state_d_C.reshape(Bs, n_odd, 2 * D_HEADS, D_QK, D_V),
                    state_d_n.reshape(Bs, n_odd, 2 * D_HEADS, D_QK),
                    jnp.broadcast_to(state_d_m.reshape(Bs, n_odd, 2 * D_HEADS, 1), (Bs, n_odd, 2 * D_HEADS, LANES)),
                    jl)
            hf, hb = _mlstm(p, gates, gates_t, hf, hb, n_seq=Bs, seq_len=T, b_off=1,
                            q_col=D_Q, k_col=D_K, v_col=D_VC, init=init)
            x, f_tok, top_i, top_g, cnt = _out_project(
                x, mods, layer, w_out, norm_ffn_g, rwh, rwl, rb, E,
                odd_in=(oc, hf, hb, p, D_O, d_norm_g[jl].reshape(1, D_HEADS * D_V)))

        x = _moe_layer(x, f_tok, top_i, top_g, cnt, mods, layer, w_gate_up, b_gate_up, w_down, b_down,
                       final_norm_g if layer == depth - 1 else None)

    y_prompt = x[0].reshape(Bp, Tp, D)
    y_sample = x[1:]
    return (y_prompt, y_sample,
            jnp.stack(new["ak"], 1), jnp.stack(new["av"], 1), jnp.stack(new["bk"], 1), jnp.stack(new["bv"], 1),
            jnp.stack(new["ck"], 1), jnp.stack(new["cv"], 1), jnp.stack(new["dC"], 1), jnp.stack(new["dn"], 1),
            jnp.stack(new["dm"], 1))
```

```python
import functools
import math

import jax
import jax.numpy as jnp
from jax import lax
from jax.experimental import pallas as pl
from jax.experimental.pallas import tpu as pltpu

HEAD_DIM = 128
GRID_W = 64
AXIS_DIM = HEAD_DIM // 2
ROPE_THETA = 10000.0
A_HEADS = 8
A_KV_HEADS = 2
B_HEADS = 4
B_VDIM = 2 * HEAD_DIM
C_HEADS = 8
C_KV_HEADS = 2
WINDOW = 128
D_HEADS = 4
D_QK = HEAD_DIM
D_V = 2 * HEAD_DIM
CHUNK = 128
MIX_WIDTH = A_HEADS * HEAD_DIM + B_HEADS * B_VDIM
AB_WIDTH = A_HEADS * HEAD_DIM + 2 * A_KV_HEADS * HEAD_DIM + 2 * B_HEADS * 2 * HEAD_DIM + B_HEADS * B_VDIM
CD_MAIN = C_HEADS * HEAD_DIM + 2 * C_KV_HEADS * HEAD_DIM + 2 * D_HEADS * D_QK + 2 * D_HEADS * D_V
N_GATES = 4 * D_HEADS
TOP_K = 4
SWIGLU_LIMIT = 7.0
SWIGLU_ALPHA = 1.702
EPS = 1e-6
NEG_INF = -1e30
LOG2E = math.log2(math.e)
LANES = 128
SUBLANES = 8
MOE_TILE = 1024
MOE_SUB = 256
VMEM_LIMIT = 56 * 1024 * 1024

_MXU = jnp.bfloat16


def _dot(a, b):
    return jnp.dot(a, b, preferred_element_type=jnp.float32)


def _dot_nt(a, b):
    return lax.dot_general(a, b, (((1,), (1,)), ((), ())), preferred_element_type=jnp.float32)


def _dot_tn(a, b):
    return lax.dot_general(a, b, (((0,), (0,)), ((), ())), preferred_element_type=jnp.float32)


def _split_hi_lo(a):
    hi = a.astype(_MXU)
    lo = (a - hi.astype(jnp.float32)).astype(_MXU)
    return hi, lo


def _pack_bf16_pair(lo, hi):
    lo_b = lax.bitcast_convert_type(lo.astype(jnp.bfloat16).astype(jnp.float32), jnp.uint32)
    hi_b = lax.bitcast_convert_type(hi.astype(jnp.bfloat16).astype(jnp.float32), jnp.uint32)
    return (hi_b & jnp.uint32(0xFFFF0000)) | (lo_b >> 16)


def _unpack_bf16_pair(w):
    lo = lax.bitcast_convert_type(w << 16, jnp.float32)
    hi = lax.bitcast_convert_type(w & jnp.uint32(0xFFFF0000), jnp.float32)
    return lo.astype(_MXU), hi.astype(_MXU)


def _sigmoid(x):
    return 1.0 / (1.0 + jnp.exp(-x))


def _log_sigmoid(x):
    return -(jnp.maximum(-x, 0.0) + jnp.log1p(jnp.exp(-jnp.abs(x))))


def _tile(n, cap):
    if n <= cap:
        return n
    t = cap - cap % LANES
    while n % t:
        t -= LANES
    return t


def _params(*sem):
    return pltpu.CompilerParams(dimension_semantics=sem, vmem_limit_bytes=VMEM_LIMIT)


def _mod_kernel(s_ref, w_ref, b_ref, o_ref):
    s = s_ref[...]
    s = s * _sigmoid(s)
    o_ref[...] = _dot(s.astype(_MXU), w_ref[...].astype(_MXU)) + b_ref[...]


def _modulation(cond_rows, w_mod, b_mod):
    depth, d, n6 = w_mod.shape
    rows = cond_rows.shape[0]
    tn = _tile(n6, 1024)
    return pl.pallas_call(
        _mod_kernel,
        out_shape=jax.ShapeDtypeStruct((depth, rows, n6), jnp.float32),
        grid=(depth, n6 // tn),
        in_specs=[
            pl.BlockSpec((rows, d), lambda l, j: (0, 0)),
            pl.BlockSpec((None, d, tn), lambda l, j: (l, 0, j)),
            pl.BlockSpec((None, 1, tn), lambda l, j: (l, 0, j)),
        ],
        out_specs=pl.BlockSpec((None, rows, tn), lambda l, j: (l, 0, j)),
        compiler_params=_params("arbitrary", "arbitrary"),
        name="modulation",
    )(cond_rows, w_mod, b_mod.reshape(depth, 1, n6))


def _rope(y, cos, sin_signed):
    lane = lax.broadcasted_iota(jnp.int32, y.shape, 1)
    first = (lane % (2 * (AXIS_DIM // 2))) < (AXIS_DIM // 2)
    partner = jnp.where(first, pltpu.roll(y, HEAD_DIM - AXIS_DIM // 2, 1), pltpu.roll(y, AXIS_DIM // 2, 1))
    return y * cos + partner * sin_signed


def _head_rms(y, g):
    return y * lax.rsqrt(jnp.mean(y * y, axis=-1, keepdims=True) + EPS) * g


def _proj_kernel(x_ref, g_ref, sh_ref, sc_ref, w_ref, cos_ref, sin_ref, qg_ref, kg_ref, *rest,
                 runs, tn, with_gates):
    if with_gates:
        wg_ref, bg_ref, o_ref, og_ref, h_sc, acc_sc = rest
    else:
        o_ref, h_sc, acc_sc = rest
    j = pl.program_id(2)
    n_tiles = runs[-1][1]

    @pl.when(j == 0)
    def _():
        x = x_ref[...]
        y = x * lax.rsqrt(jnp.mean(x * x, axis=-1, keepdims=True) + EPS) * g_ref[...]
        h = y * (1.0 + sc_ref[...]) + sh_ref[...]
        h_sc[...] = h.astype(_MXU)
        if with_gates:
            og_ref[...] = _dot(h_sc[...], wg_ref[...].astype(_MXU)) + bg_ref[...]
        acc_sc[...] = _dot(h_sc[...], w_ref[...].astype(_MXU))
        o_ref[...] = jnp.zeros(o_ref.shape, o_ref.dtype)

    def epilogue(ops, prev):
        for t, (norm, rope, scale) in enumerate(ops):
            y = prev[:, t * HEAD_DIM:(t + 1) * HEAD_DIM]
            if norm == "q":
                y = _head_rms(y, qg_ref[...])
            elif norm == "k":
                y = _head_rms(y, kg_ref[...])
            if rope:
                y = _rope(y, cos_ref[...], sin_ref[...])
            if scale != 1.0:
                y = y * scale
            o_ref[:, t * HEAD_DIM:(t + 1) * HEAD_DIM] = y

    for lo, hi, ops in runs:
        @pl.when((j - 1 >= lo) & (j - 1 < hi) & (j < n_tiles))
        def _(ops=ops):
            prev = acc_sc[...]
            nxt = _dot(h_sc[...], w_ref[...].astype(_MXU))
            epilogue(ops, prev)
            acc_sc[...] = nxt

    @pl.when(j == n_tiles)
    def _():
        epilogue(runs[-1][2], acc_sc[...])


def _tile_runs(head_ops, tn):
    per = tn // HEAD_DIM
    tiles = [tuple(head_ops[i * per:(i + 1) * per]) for i in range(len(head_ops) // per)]
    runs, lo = [], 0
    for i in range(1, len(tiles) + 1):
        if i == len(tiles) or tiles[i] != tiles[lo]:
            runs.append((lo, i, tiles[lo]))
            lo = i
    return tuple(runs)


def _project(x, mods, layer, norm_g, w_in, jl, n_main, head_ops, cos_t, sin_t, qg, kg, gates=None):
    G, T, D = x.shape
    tm = min(1024, T)
    tn = _tile(n_main, 512)
    runs = _tile_runs(head_ops, tn)
    n_tiles = n_main // tn
    with_gates = gates is not None
    in_specs = [
        pl.BlockSpec((None, tm, D), lambda b, i, j: (b, i, 0)),
        pl.BlockSpec((None, 1, D), lambda b, i, j: (layer, 0, 0)),
        pl.BlockSpec((None, None, None, 1, D), lambda b, i, j: (layer, 0, b, 0, 0)),
        pl.BlockSpec((None, None, None, 1, D), lambda b, i, j: (layer, 1, b, 0, 0)),
        pl.BlockSpec((None, D, tn), lambda b, i, j: (jl, 0, jnp.minimum(j, n_tiles - 1))),
        pl.BlockSpec((None, tm, HEAD_DIM), lambda b, i, j: (jnp.minimum(b, 1), i, 0)),
        pl.BlockSpec((None, tm, HEAD_DIM), lambda b, i, j: (jnp.minimum(b, 1), i, 0)),
        pl.BlockSpec((1, HEAD_DIM), lambda b, i, j: (0, 0)),
        pl.BlockSpec((1, HEAD_DIM), lambda b, i, j: (0, 0)),
    ]
    args = [x, norm_g.reshape(norm_g.shape[0], 1, D), mods, mods, w_in, cos_t, sin_t, qg, kg]
    out_shape = [jax.ShapeDtypeStruct((G, T, n_main), jnp.float32)]
    out_specs = [pl.BlockSpec((None, tm, tn), lambda b, i, j: (b, i, jnp.maximum(j - 1, 0)))]
    if with_gates:
        wg, bg = gates
        in_specs += [pl.BlockSpec((D, LANES), lambda b, i, j: (0, 0)),
                     pl.BlockSpec((1, LANES), lambda b, i, j: (0, 0))]
        args += [wg, bg]
        out_shape.append(jax.ShapeDtypeStruct((G, T, LANES), jnp.float32))
        out_specs.append(pl.BlockSpec((None, tm, LANES), lambda b, i, j: (b, i, 0)))
    res = pl.pallas_call(
        functools.partial(_proj_kernel, runs=runs, tn=tn, with_gates=with_gates),
        out_shape=out_shape,
        grid=(G, T // tm, n_tiles + 1),
        in_specs=in_specs,
        out_specs=out_specs,
        scratch_shapes=[pltpu.VMEM((tm, D), _MXU), pltpu.VMEM((tm, tn), jnp.float32)],
        compiler_params=_params("arbitrary", "arbitrary", "arbitrary"),
        name="norm_mod_project",
    )(*args)
    return res if with_gates else res[0]


def _flash_kernel(*refs, groups, tq, tk, n_lat, dv, has_cache, has_sink, diff, lam_init):
    it = iter(refs)
    q_ref, k_ref, v_ref = next(it), next(it), next(it)
    ck_refs = [next(it) for _ in groups] if has_cache else []
    cv_ref = next(it) if has_cache else None
    sink_ref = next(it) if has_sink else None
    if diff:
        lam_ref, subg_ref = next(it), next(it)
    o_ref = next(it)
    scr = [(next(it), next(it), next(it)) for _ in groups]
    kvh = pl.program_id(1)

    qs = []
    for gi, (q_offs, k_off) in enumerate(groups):
        q = jnp.concatenate([q_ref[:, o:o + HEAD_DIM] for o in q_offs], axis=0).astype(_MXU)
        qs.append(q)
        m_sc, l_sc, acc_sc = scr[gi]
        if has_sink:
            n_h = len(q_offs)
            m_sc[...] = jnp.concatenate(
                [jnp.full((tq, LANES), sink_ref[kvh * n_h + t] * LOG2E, jnp.float32) for t in range(n_h)], axis=0)
            lane = lax.broadcasted_iota(jnp.int32, l_sc.shape, 1)
            l_sc[...] = jnp.where(lane == 0, 1.0, 0.0)
        else:
            m_sc[...] = jnp.full(m_sc.shape, NEG_INF, jnp.float32)
            l_sc[...] = jnp.zeros(l_sc.shape, jnp.float32)
        acc_sc[...] = jnp.zeros(acc_sc.shape, jnp.float32)

    def update(gi, k, v):
        m_sc, l_sc, acc_sc = scr[gi]
        s = _dot_nt(qs[gi], k)
        tiles = [s[:, c * LANES:(c + 1) * LANES] for c in range(s.shape[1] // LANES)]
        mx = functools.reduce(jnp.maximum, tiles)
        m_prev = m_sc[...]
        m_new = jnp.maximum(m_prev, jnp.max(mx, axis=-1, keepdims=True))
        alpha = jnp.exp2(m_prev - m_new)
        ps = [jnp.exp2(t - m_new) for t in tiles]
        l_sc[...] = alpha * l_sc[...] + functools.reduce(jnp.add, ps)
        pv = _dot(jnp.concatenate(ps, axis=-1).astype(_MXU), v)
        for c in range(dv // LANES):
            cs = slice(c * LANES, (c + 1) * LANES)
            acc_sc[:, cs] = alpha * acc_sc[:, cs] + pv[:, cs]
        m_sc[...] = m_new

    def chunk(c, carry):
        r0 = pl.multiple_of(c * tk, tk)
        v = v_ref[pl.ds(r0, tk), :].astype(_MXU)
        for gi, (q_offs, k_off) in enumerate(groups):
            k = k_ref[pl.ds(r0, tk), k_off:k_off + HEAD_DIM].astype(_MXU)
            update(gi, k, v)
        return carry

    lax.fori_loop(0, n_lat // tk, chunk, 0)
    if has_cache:
        v = cv_ref[...].astype(_MXU)
        for gi in range(len(groups)):
            update(gi, ck_refs[gi][...].astype(_MXU), v)

    outs = []
    for gi in range(len(groups)):
        m_sc, l_sc, acc_sc = scr[gi]
        outs.append(acc_sc[...] / jnp.sum(l_sc[...], axis=-1, keepdims=True))
    if diff:
        l32 = lam_ref[...]
        lam = (jnp.exp(jnp.sum(l32[0:1] * l32[1:2], axis=-1, keepdims=True))
               - jnp.exp(jnp.sum(l32[2:3] * l32[3:4], axis=-1, keepdims=True)) + lam_init)
        ob = outs[0] - lam * outs[1]
        ob = ob * lax.rsqrt(jnp.mean(ob * ob, axis=-1, keepdims=True) + EPS) * subg_ref[...]
        o_ref[...] = (ob * (1.0 - lam_init)).astype(o_ref.dtype)
    else:
        o = outs[0]
        for t in range(len(groups[0][0])):
            o_ref[:, t * dv:(t + 1) * dv] = o[t * tq:(t + 1) * tq].astype(o_ref.dtype)


def _attention(p_arr, o_arr, *, n_seq, seq_len, b_off, n_kvh, groups, q_col, q_w, k_col, k_w, v_col, dv,
               o_col, o_w, tq, cache=None, sink=None, diff=None, name):
    G, T, _ = p_arr.shape
    per_b = T // seq_len
    nq = seq_len // tq
    tk = min(1024, seq_len)

    def row_map(s, i, blocks_per_seq):
        return b_off + s // per_b, (s % per_b) * blocks_per_seq + i

    def q_map(s, h, i):
        b, r = row_map(s, i, nq)
        return b, r, q_col // q_w + h

    def k_map(s, h, i):
        b, r = row_map(s, 0, 1)
        return b, r, k_col // k_w + h

    def v_map(s, h, i):
        b, r = row_map(s, 0, 1)
        return b, r, v_col // dv + h

    def o_map(s, h, i):
        b, r = row_map(s, i, nq)
        return b, r, o_col // o_w + h

    in_specs = [pl.BlockSpec((None, tq, q_w), q_map),
                pl.BlockSpec((None, seq_len, k_w), k_map),
                pl.BlockSpec((None, seq_len, dv), v_map)]
    args = [p_arr, p_arr, p_arr]
    if cache is not None:
        ck_list, cv = cache
        for arr, spec in ck_list:
            in_specs.append(spec)
            args.append(arr)
        in_specs.append(cv[1])
        args.append(cv[0])
    if sink is not None:
        in_specs.append(pl.BlockSpec(memory_space=pltpu.SMEM))
        args.append(sink)
    lam_init = 0.0
    if diff is not None:
        lam_arr, subg, lam_init = diff
        in_specs += [pl.BlockSpec((4, HEAD_DIM), lambda s, h, i: (0, 0)),
                     pl.BlockSpec((1, dv), lambda s, h, i: (0, 0))]
        args += [lam_arr, subg]
    n_in = len(args)
    in_specs.append(pl.BlockSpec((None, tq, o_w), o_map))
    args.append(o_arr)
    scratch = []
    for q_offs, _ in groups:
        m = len(q_offs) * tq
        scratch += [pltpu.VMEM((m, LANES), jnp.float32), pltpu.VMEM((m, LANES), jnp.float32),
                    pltpu.VMEM((m, dv), jnp.float32)]

    def body(*refs):
        refs = refs[:n_in] + refs[n_in + 1:]
        _flash_kernel(*refs, groups=groups, tq=tq, tk=tk, n_lat=seq_len, dv=dv,
                      has_cache=cache is not None, has_sink=sink is not None,
                      diff=diff is not None, lam_init=lam_init)

    return pl.pallas_call(
        body,
        out_shape=jax.ShapeDtypeStruct(o_arr.shape, o_arr.dtype),
        grid=(n_seq, n_kvh, nq),
        in_specs=in_specs,
        out_specs=pl.BlockSpec((None, tq, o_w), o_map),
        scratch_shapes=scratch,
        input_output_aliases={n_in: 0},
        compiler_params=_params("arbitrary", "arbitrary", "arbitrary"),
        name=name,
    )(*args)


def _window_kernel(q_ref, k_ref, v_ref, ck_ref, cv_ref, sink_ref, o_ref, *, tq, n_h, seq_len):
    kvh = pl.program_id(1)
    i = pl.program_id(2)
    span = tq + 2 * WINDOW
    start = jnp.clip(i * tq - WINDOW, 0, seq_len - span)
    start = pl.multiple_of(start, WINDOW)
    k = k_ref[pl.ds(start, span), :].astype(_MXU)
    v = v_ref[pl.ds(start, span), :].astype(_MXU)
    q = jnp.concatenate([q_ref[:, t * HEAD_DIM:(t + 1) * HEAD_DIM] for t in range(n_h)], axis=0).astype(_MXU)
    m_rows = n_h * tq
    s_loc = _dot_nt(q, k)
    row = lax.broadcasted_iota(jnp.int32, (m_rows, span), 0)
    col = lax.broadcasted_iota(jnp.int32, (m_rows, span), 1)
    qpos = i * tq + (row & (tq - 1))
    kpos = start + col
    s_loc = jnp.where(jnp.abs(qpos - kpos) <= WINDOW, s_loc, NEG_INF)
    s_ctx = _dot_nt(q, ck_ref[...].astype(_MXU))
    sk = jnp.concatenate(
        [jnp.full((tq, LANES), sink_ref[kvh * n_h + t] * LOG2E, jnp.float32) for t in range(n_h)], axis=0)
    loc_tiles = [s_loc[:, c * LANES:(c + 1) * LANES] for c in range(span // LANES)]
    ctx_tiles = [s_ctx[:, c * LANES:(c + 1) * LANES] for c in range(s_ctx.shape[1] // LANES)]
    mx = functools.reduce(jnp.maximum, loc_tiles + ctx_tiles)
    m = jnp.maximum(jnp.max(mx, axis=-1, keepdims=True), sk)
    p_loc = [jnp.exp2(t - m) for t in loc_tiles]
    p_ctx = [jnp.exp2(t - m) for t in ctx_tiles]
    den = (jnp.sum(functools.reduce(jnp.add, p_loc + p_ctx), axis=-1, keepdims=True)
           + jnp.exp2(sk - m)[:, 0:1])
    o = (_dot(jnp.concatenate(p_loc, axis=-1).astype(_MXU), v)
         + _dot(jnp.concatenate(p_ctx, axis=-1).astype(_MXU), cv_ref[...].astype(_MXU))) / den
    for t in range(n_h):
        o_ref[:, t * HEAD_DIM:(t + 1) * HEAD_DIM] = o[t * tq:(t + 1) * tq].astype(o_ref.dtype)


def _window_attention(p_arr, o_arr, cache_k, cache_v, jl, sink, *, b_off, n_b, q_col, k_col, v_col):
    G, T, _ = p_arr.shape
    n_h = C_HEADS // C_KV_HEADS
    q_w = n_h * HEAD_DIM
    tq = min(256, T - 2 * WINDOW)
    assert tq & (tq - 1) == 0 and T % tq == 0
    past = cache_k.shape[-2]
    in_specs = [
        pl.BlockSpec((None, tq, q_w), lambda b, h, i: (b + b_off, i, q_col // q_w + h)),
        pl.BlockSpec((None, T, HEAD_DIM), lambda b, h, i: (b + b_off, 0, k_col // HEAD_DIM + h)),
        pl.BlockSpec((None, T, HEAD_DIM), lambda b, h, i: (b + b_off, 0, v_col // HEAD_DIM + h)),
        pl.BlockSpec((None, None, None, past, HEAD_DIM), lambda b, h, i: (b, jl, h, 0, 0)),
        pl.BlockSpec((None, None, None, past, HEAD_DIM), lambda b, h, i: (b, jl, h, 0, 0)),
        pl.BlockSpec(memory_space=pltpu.SMEM),
        pl.BlockSpec((None, tq, q_w), lambda b, h, i: (b + b_off, i, h)),
    ]

    def body(q_ref, k_ref, v_ref, ck_ref, cv_ref, sink_ref, _o_in, o_ref):
        _window_kernel(q_ref, k_ref, v_ref, ck_ref, cv_ref, sink_ref, o_ref, tq=tq, n_h=n_h, seq_len=T)

    return pl.pallas_call(
        body,
        out_shape=jax.ShapeDtypeStruct(o_arr.shape, o_arr.dtype),
        grid=(n_b, C_KV_HEADS, T // tq),
        in_specs=in_specs,
        out_specs=pl.BlockSpec((None, tq, q_w), lambda b, h, i: (b + b_off, i, h)),
        input_output_aliases={6: 0},
        compiler_params=_params("arbitrary", "arbitrary", "arbitrary"),
        name="window_attention",
    )(p_arr, p_arr, p_arr, cache_k, cache_v, sink, o_arr)


def _mlstm_kernel(*refs, nc, has_init, emit_state):
    it = iter(refs)
    qf, kf, vf0, vf1, qb, kb, vb0, vb1 = [next(it) for _ in range(8)]
    gcf, grf, gcb, grb = [next(it) for _ in range(4)]
    if has_init:
        c0_ref, n0_ref, m0_ref = next(it), next(it), next(it)
    _hf_in, _hb_in = next(it), next(it)
    hf_ref, hb_ref = next(it), next(it)
    if emit_state:
        cout_ref, nout_ref, mout_ref = next(it), next(it), next(it)
    c_sc, n_sc, m_sc = next(it), next(it), next(it)
    j = pl.program_id(1)
    L = CHUNK
    nch = 2 * D_HEADS

    @pl.when(j == 0)
    def _():
        if has_init:
            c_sc[...] = c0_ref[...]
            n_sc[...] = n0_ref[...]
            m_sc[...] = m0_ref[...]
        else:
            c_sc[...] = jnp.zeros(c_sc.shape, jnp.float32)
            n_sc[...] = jnp.zeros(n_sc.shape, jnp.float32)
            m_sc[...] = jnp.zeros(m_sc.shape, jnp.float32)

    row = lax.broadcasted_iota(jnp.int32, (L, L), 0)
    col = lax.broadcasted_iota(jnp.int32, (L, L), 1)
    lower = col <= row
    upper = col >= row
    lower_m = jnp.where(lower, 1.0, 0.0).astype(_MXU)
    upper_m = jnp.where(upper, 1.0, 0.0).astype(_MXU)

    def exact_dot(a, b, a_is_data):
        if a_is_data:
            hi, lo = _split_hi_lo(a)
            return _dot(hi, b) + _dot(lo, b)
        hi, lo = _split_hi_lo(b)
        return _dot(a, hi) + _dot(a, lo)

    for d in range(2):
        rev = d == 1
        q_ref, k_ref, v_refs = (qb, kb, (vb0, vb1)) if rev else (qf, kf, (vf0, vf1))
        gc_ref, gr_ref = (gcb, grb) if rev else (gcf, grf)
        h_ref = hb_ref if rev else hf_ref
        gc = gc_ref[...]
        gr = gr_ref[...]
        lf_c = _log_sigmoid(gc)
        lf_r = _log_sigmoid(gr)
        b_cols = exact_dot(upper_m if rev else lower_m, lf_c, False)
        b_rows = exact_dot(lf_r, lower_m if rev else upper_m, True)
        tri = upper if rev else lower
        for hd in range(D_HEADS):
            ci = d * D_HEADS + hd
            ic, fc = d * 2 * D_HEADS + hd, d * 2 * D_HEADS + D_HEADS + hd
            ig_c, ig_r = gc[:, ic:ic + 1], gr[ic:ic + 1, :]
            b_c, b_r = b_cols[:, fc:fc + 1], b_rows[fc:fc + 1, :]
            q = q_ref[:, hd * D_QK:(hd + 1) * D_QK]
            k = k_ref[:, hd * D_QK:(hd + 1) * D_QK]
            v_ref = v_refs[hd // 2]
            v = v_ref[:, (hd % 2) * D_V:(hd % 2 + 1) * D_V]
            m = m_sc[ci:ci + 1, 0:1]
            n = n_sc[ci:ci + 1, :]
            c_mat = c_sc[ci]
            qm, km, vm = q.astype(_MXU), k.astype(_MXU), v.astype(_MXU)

            dmat = jnp.where(tri, b_c - b_r + ig_r, NEG_INF)
            m_prev = b_c + m
            m_t = jnp.maximum(m_prev, jnp.max(dmat, axis=-1, keepdims=True))
            w_intra = jnp.exp(dmat - m_t)
            w_prev = jnp.exp(m_prev - m_t)
            s = _dot_nt(qm, km) * w_intra
            num = w_prev * _dot(qm, c_mat.astype(_MXU)) + _dot(s.astype(_MXU), vm)
            den = w_prev * jnp.sum(q * n, axis=-1, keepdims=True) + jnp.sum(s, axis=-1, keepdims=True)
            h = num / jnp.maximum(jnp.abs(den), jnp.exp(-m_t))
            h_ref[:, hd * D_V:(hd + 1) * D_V] = h.astype(h_ref.dtype)

            b_end = b_c[0:1] if rev else b_c[L - 1:L]
            g_c = b_end - b_c + ig_c
            g_r = b_end - b_r + ig_r
            m_new = jnp.maximum(b_end + m, jnp.max(g_r, axis=-1, keepdims=True))
            wk = jnp.exp(g_c - m_new)
            decay = jnp.exp(b_end + m - m_new)
            c_sc[ci] = decay * c_mat + _dot_tn(km, (wk * v).astype(_MXU))
            n_sc[ci:ci + 1, :] = decay * n + jnp.sum(wk * k, axis=0, keepdims=True)
            m_sc[ci:ci + 1, :] = jnp.broadcast_to(m_new, (1, LANES))

    if emit_state:
        @pl.when(j == nc - 1)
        def _():
            cout_ref[...] = c_sc[...]
            nout_ref[...] = n_sc[...]
            mout_ref[...] = m_sc[...]


def _mlstm(p_arr, gates, gates_t, hf_arr, hb_arr, *, n_seq, seq_len, b_off, q_col, k_col, v_col,
           init=None, emit_state=False):
    G, T, _ = p_arr.shape
    per_b = T // seq_len
    nc = seq_len // CHUNK
    wq = D_HEADS * D_QK
    hw = D_HEADS * D_V
    nch = 2 * D_HEADS

    def rows(s, j, rev):
        jj = nc - 1 - j if rev else j
        return b_off + s // per_b, (s % per_b) * nc + jj

    def col_spec(width, col, rev):
        return pl.BlockSpec((None, CHUNK, width), lambda s, j: rows(s, j, rev) + (col // width,))

    def gate_t_spec(rev):
        def imap(s, j):
            b, r = rows(s, j, rev)
            return b, 0, r
        return pl.BlockSpec((None, N_GATES, CHUNK), imap)

    in_specs, args = [], []
    for rev in (False, True):
        in_specs += [col_spec(wq, q_col, rev), col_spec(wq, k_col, rev),
                     col_spec(wq, v_col, rev), col_spec(wq, v_col + wq, rev)]
        args += [p_arr] * 4
    for rev in (False, True):
        in_specs += [col_spec(LANES, 0, rev), gate_t_spec(rev)]
        args += [gates, gates_t]
    if init is not None:
        c0, n0, m0, jl = init
        in_specs += [pl.BlockSpec((None, None, nch, D_QK, D_V), lambda s, j: (s, jl, 0, 0, 0)),
                     pl.BlockSpec((None, None, nch, D_QK), lambda s, j: (s, jl, 0, 0)),
                     pl.BlockSpec((None, None, nch, LANES), lambda s, j: (s, jl, 0, 0))]
        args += [c0, n0, m0]
    n_in = len(args)
    h_specs = [col_spec(hw, 0, False), col_spec(hw, 0, True)]
    in_specs += h_specs
    args += [hf_arr, hb_arr]
    out_shape = [jax.ShapeDtypeStruct(hf_arr.shape, hf_arr.dtype),
                 jax.ShapeDtypeStruct(hb_arr.shape, hb_arr.dtype)]
    out_specs = list(h_specs)
    if emit_state:
        out_shape += [jax.ShapeDtypeStruct((n_seq, nch, D_QK, D_V), jnp.float32),
                      jax.ShapeDtypeStruct((n_seq, nch, D_QK), jnp.float32),
                      jax.ShapeDtypeStruct((n_seq, nch, LANES), jnp.float32)]
        out_specs += [pl.BlockSpec((None, nch, D_QK, D_V), lambda s, j: (s, 0, 0, 0)),
                      pl.BlockSpec((None, nch, D_QK), lambda s, j: (s, 0, 0)),
                      pl.BlockSpec((None, nch, LANES), lambda s, j: (s, 0, 0))]
    return pl.pallas_call(
        functools.partial(_mlstm_kernel, nc=nc, has_init=init is not None, emit_state=emit_state),
        out_shape=out_shape,
        grid=(n_seq, nc),
        in_specs=in_specs,
        out_specs=out_specs,
        scratch_shapes=[pltpu.VMEM((nch, D_QK, D_V), jnp.float32), pltpu.VMEM((nch, D_QK), jnp.float32),
                        pltpu.VMEM((nch, LANES), jnp.float32)],
        input_output_aliases={n_in: 0, n_in + 1: 1},
        compiler_params=_params("arbitrary", "arbitrary"),
        name="mlstm",
    )(*args)


def _outproj_kernel(*refs, odd, nj, tn, n_experts):
    it = iter(refs)
    if odd:
        oc_ref, hf_ref, hb_ref, do0_ref, do1_ref, dng_ref = [next(it) for _ in range(6)]
    else:
        o_ref = next(it)
    x_ref, g1_ref, w_ref, ng_ref, sh2_ref, sc2_ref, rwh_ref, rwl_ref, rb_ref = [next(it) for _ in range(9)]
    xn_ref, f_ref, ti_ref, tg_ref, cnt_ref = [next(it) for _ in range(5)]
    xn_sc, cnt_sc = next(it), next(it)
    om_sc = next(it) if odd else None
    j = pl.program_id(2)
    first_tile = (pl.program_id(0) == 0) & (pl.program_id(1) == 0)

    if odd:
        @pl.when(j == 0)
        def _():
            wc = oc_ref.shape[1]
            om_sc[:, :wc] = oc_ref[...].astype(_MXU)
            hd = hf_ref[...] + hb_ref[...]
            half = do0_ref.shape[1]
            for hh in range(D_HEADS):
                y = hd[:, hh * D_V:(hh + 1) * D_V]
                y = y * lax.rsqrt(jnp.mean(y * y, axis=-1, keepdims=True) + EPS) * dng_ref[:, hh * D_V:(hh + 1) * D_V]
                do_ref = do0_ref if hh * D_V < half else do1_ref
                c0 = (hh * D_V) % half
                y = y * _sigmoid(do_ref[:, c0:c0 + D_V])
                om_sc[:, wc + hh * D_V:wc + (hh + 1) * D_V] = y.astype(_MXU)
        o = om_sc[...]
    else:
        o = o_ref[...].astype(_MXU)

    xn = x_ref[...] + g1_ref[...] * _dot(o, w_ref[...].astype(_MXU))
    xn_ref[...] = xn
    xn_sc[j] = xn

    @pl.when(j == nj - 1)
    def _():
        ss = jnp.zeros((xn.shape[0], 1), jnp.float32)
        for c in range(nj):
            t = xn_sc[c]
            ss = ss + jnp.sum(t * t, axis=-1, keepdims=True)
        inv = lax.rsqrt(ss / (nj * tn) + EPS)
        logits = rb_ref[...]
        for c in range(nj):
            cs = slice(c * tn, (c + 1) * tn)
            f = (xn_sc[c] * inv * ng_ref[:, cs]) * (1.0 + sc2_ref[:, cs]) + sh2_ref[:, cs]
            xn_sc[c] = f
            fh, fl = _split_hi_lo(f)
            logits = logits + (_dot(fh, rwh_ref[cs, :]) + _dot(fl, rwh_ref[cs, :]) + _dot(fh, rwl_ref[cs, :]))
        half = nj * tn // 2
        pw = min(tn, half)
        for pc in range(half // pw):
            lo_c, hi_c = pc * pw, half + pc * pw
            lo = xn_sc[lo_c // tn][:, lo_c % tn:lo_c % tn + pw]
            hi = xn_sc[hi_c // tn][:, hi_c % tn:hi_c % tn + pw]
            f_ref[:, pc * pw:(pc + 1) * pw] = _pack_bf16_pair(lo, hi)
        lane = lax.broadcasted_iota(jnp.int32, logits.shape, 1)
        lane_f = lane.astype(jnp.float32)
        cur = logits
        vals, idxs = [], []
        for _ in range(TOP_K):
            mx = jnp.max(cur, axis=-1, keepdims=True)
            ix = jnp.min(jnp.where(cur == mx, lane_f, float(LANES)), axis=-1, keepdims=True).astype(jnp.int32)
            vals.append(mx)
            idxs.append(ix)
            cur = jnp.where(lane == ix, -jnp.inf, cur)
        es = [jnp.exp(v - vals[0]) for v in vals]
        tot = es[0] + es[1] + es[2] + es[3]
        ti = jnp.zeros(logits.shape, jnp.int32)
        tg = jnp.zeros(logits.shape, jnp.float32)
        for kk in range(TOP_K):
            ti = jnp.where(lane == kk, idxs[kk], ti)
            tg = jnp.where(lane == kk, es[kk] / tot, tg)
        tg_ref[...] = tg

        @pl.when(first_tile)
        def _():
            cnt_sc[...] = jnp.zeros(cnt_sc.shape, jnp.float32)

        tm = logits.shape[0]
        onehot = [jnp.where(lane == idxs[kk], 1.0, 0.0) for kk in range(TOP_K)]
        picked = functools.reduce(jnp.add, onehot)
        r_i = lax.broadcasted_iota(jnp.int32, (tm, tm), 0)
        c_i = lax.broadcasted_iota(jnp.int32, (tm, tm), 1)
        earlier = jnp.where(c_i < r_i, 1.0, 0.0).astype(_MXU)
        before = cnt_sc[...] + _dot(earlier, picked.astype(_MXU))
        for kk in range(TOP_K):
            rank = jnp.sum(onehot[kk] * before, axis=-1, keepdims=True)
            ti = jnp.where(lane == TOP_K + kk, rank.astype(jnp.int32), ti)
        ti_ref[...] = ti
        cnt_sc[...] += jnp.sum(picked, axis=0, keepdims=True)
        cnt_ref[...] = cnt_sc[...]


def _out_project(x, mods, layer, w_out, norm_ffn_g, rwh, rwl, rb, n_experts, *, o=None, odd_in=None):
    G, T, D = x.shape
    mix = w_out.shape[1]
    tm = min(512, T)
    tn = _tile(D, 512)
    nj = D // tn
    odd = odd_in is not None
    bmap = lambda b, i, j: (b, i, 0)
    if odd:
        oc, hf, hb, p_cd, do_col, d_norm_g = odd_in
        half = (D_HEADS * D_V) // 2
        in_specs = [pl.BlockSpec((None, tm, oc.shape[-1]), bmap),
                    pl.BlockSpec((None, tm, hf.shape[-1]), bmap),
                    pl.BlockSpec((None, tm, hb.shape[-1]), bmap),
                    pl.BlockSpec((None, tm, half), lambda b, i, j: (b, i, do_col // half)),
                    pl.BlockSpec((None, tm, half), lambda b, i, j: (b, i, do_col // half + 1)),
                    pl.BlockSpec((1, D_HEADS * D_V), lambda b, i, j: (0, 0))]
        args = [oc, hf, hb, p_cd, p_cd, d_norm_g]
    else:
        in_specs = [pl.BlockSpec((None, tm, mix), bmap)]
        args = [o]
    in_specs += [
        pl.BlockSpec((None, tm, tn), lambda b, i, j: (b, i, j)),
        pl.BlockSpec((None, None, None, 1, tn), lambda b, i, j: (layer, 2, b, 0, j)),
        pl.BlockSpec((None, mix, tn), lambda b, i, j: (layer, 0, j)),
        pl.BlockSpec((None, 1, D), lambda b, i, j: (layer, 0, 0)),
        pl.BlockSpec((None, None, None, 1, D), lambda b, i, j: (layer, 3, b, 0, 0)),
        pl.BlockSpec((None, None, None, 1, D), lambda b, i, j: (layer, 4, b, 0, 0)),
        pl.BlockSpec((None, D, LANES), lambda b, i, j: (layer, 0, 0)),
        pl.BlockSpec((None, D, LANES), lambda b, i, j: (layer, 0, 0)),
        pl.BlockSpec((None, 1, LANES), lambda b, i, j: (layer, 0, 0)),
    ]
    args += [x, mods, w_out, norm_ffn_g.reshape(norm_ffn_g.shape[0], 1, D), mods, mods, rwh, rwl, rb]
    out_shape = [jax.ShapeDtypeStruct((G, T, D), jnp.float32),
                 jax.ShapeDtypeStruct((G, T, D // 2), jnp.uint32),
                 jax.ShapeDtypeStruct((G, T, LANES), jnp.int32),
                 jax.ShapeDtypeStruct((G, T, LANES), jnp.float32),
                 jax.ShapeDtypeStruct((1, LANES), jnp.float32)]
    out_specs = [pl.BlockSpec((None, tm, tn), lambda b, i, j: (b, i, j)),
                 pl.BlockSpec((None, tm, D // 2), bmap),
                 pl.BlockSpec((None, tm, LANES), bmap),
                 pl.BlockSpec((None, tm, LANES), bmap),
                 pl.BlockSpec((1, LANES), lambda b, i, j: (0, 0))]
    scratch = [pltpu.VMEM((nj, tm, tn), jnp.float32), pltpu.VMEM((1, LANES), jnp.float32)]
    if odd:
        scratch.append(pltpu.VMEM((tm, mix), _MXU))
    return pl.pallas_call(
        functools.partial(_outproj_kernel, odd=odd, nj=nj, tn=tn, n_experts=n_experts),
        out_shape=out_shape,
        grid=(G, T // tm, nj),
        in_specs=in_specs,
        out_specs=out_specs,
        scratch_shapes=scratch,
        compiler_params=_params("arbitrary", "arbitrary", "arbitrary"),
        name="out_project_router",
    )(*args)


def _moe_kernel(te_ref, ns_ref, x_ref, wg_ref, wu_ref, wd_ref, bg_ref, bu_ref, bd_ref, o_ref, *, n_sub):
    t = pl.program_id(0)
    c = pl.program_id(1)
    nsub = ns_ref[t]
    half = x_ref.shape[1]

    def expert_rows(row_slices):
        wg_lo, wg_hi = wg_ref[:half, :].astype(_MXU), wg_ref[half:, :].astype(_MXU)
        wu_lo, wu_hi = wu_ref[:half, :].astype(_MXU), wu_ref[half:, :].astype(_MXU)
        wd = wd_ref[...].astype(_MXU)
        parts = []
        for rs in row_slices:
            lo, hi = _unpack_bf16_pair(x_ref[rs, :])
            gt = (_dot(lo, wg_lo) + _dot(hi, wg_hi)) + bg_ref[...]
            up = (_dot(lo, wu_lo) + _dot(hi, wu_hi)) + bu_ref[...]
            gt = jnp.minimum(gt, SWIGLU_LIMIT)
            up = jnp.clip(up, -SWIGLU_LIMIT, SWIGLU_LIMIT)
            act = gt * _sigmoid(SWIGLU_ALPHA * gt) * (up + 1.0)
            parts.append(_dot(act.astype(_MXU), wd))

        @pl.when(c == 0)
        def _():
            for rs, part in zip(row_slices, parts):
                o_ref[rs, :] = part + bd_ref[...]

        @pl.when(c > 0)
        def _():
            for rs, part in zip(row_slices, parts):
                o_ref[rs, :] += part

    @pl.when(nsub == n_sub)
    def _():
        expert_rows([slice(0, n_sub * MOE_SUB)])

    for s in range(n_sub):
        rs = slice(s * MOE_SUB, (s + 1) * MOE_SUB)

        @pl.when((s < nsub) & (nsub < n_sub))
        def _(rs=rs):
            expert_rows([rs])

        @pl.when((s >= nsub) & (c == 0))
        def _(rs=rs):
            o_ref[rs, :] = jnp.zeros((MOE_SUB, o_ref.shape[1]), jnp.float32)


def _moe_experts(xb, tile_expert, tile_nsub, w_gate_up, b_gate_up, w_down, b_down, layer):
    P, half = xb.shape
    E, D, F2 = w_gate_up.shape[1:]
    F = F2 // 2
    fc = _tile(F, 512)
    nfc = F // fc
    n_tiles = P // MOE_TILE
    grid_spec = pltpu.PrefetchScalarGridSpec(
        num_scalar_prefetch=2,
        grid=(n_tiles, nfc),
        in_specs=[
            pl.BlockSpec((MOE_TILE, half), lambda t, c, te, ns: (t, 0)),
            pl.BlockSpec((None, None, D, fc), lambda t, c, te, ns: (layer, te[t], 0, c)),
            pl.BlockSpec((None, None, D, fc), lambda t, c, te, ns: (layer, te[t], 0, nfc + c)),
            pl.BlockSpec((None, None, fc, D), lambda t, c, te, ns: (layer, te[t], c, 0)),
            pl.BlockSpec((None, None, 1, fc), lambda t, c, te, ns: (layer, te[t], 0, c)),
            pl.BlockSpec((None, None, 1, fc), lambda t, c, te, ns: (layer, te[t], 0, nfc + c)),
            pl.BlockSpec((None, None, 1, D), lambda t, c, te, ns: (layer, te[t], 0, 0)),
        ],
        out_specs=pl.BlockSpec((MOE_TILE, D), lambda t, c, te, ns: (t, 0)),
    )
    depth = w_gate_up.shape[0]
    return pl.pallas_call(
        functools.partial(_moe_kernel, n_sub=MOE_TILE // MOE_SUB),
        out_shape=jax.ShapeDtypeStruct((P, D), jnp.float32),
        grid_spec=grid_spec,
        compiler_params=_params("arbitrary", "arbitrary"),
        name="moe_experts",
    )(tile_expert, tile_nsub, xb, w_gate_up, w_gate_up, w_down,
      b_gate_up.reshape(depth, E, 1, F2), b_gate_up.reshape(depth, E, 1, F2), b_down.reshape(depth, E, 1, D))


def _dispatch_kernel(fs_ref, fl_ref, nu_ref, pos_ref, f_ref, xb_ref, zero_sc, sem, zsem, *, tt, n_experts, n_tiles):
    @pl.when(pl.program_id(0) == 0)
    def _():
        zero_sc[...] = jnp.zeros(zero_sc.shape, zero_sc.dtype)

        def zero_copy(off, size):
            return pltpu.make_async_copy(zero_sc.at[pl.ds(0, size)], xb_ref.at[pl.ds(off, size)], zsem)

        def pad_pieces(e, act):
            n, start = fl_ref[e], fs_ref[e]
            head = (-start) & (SUBLANES - 1)
            for h in range(SUBLANES - 1):
                @pl.when(h < head)
                def _(h=h):
                    act(zero_copy(start + h, 1))
            rest = n - head
            off = start + head
            for b in reversed(range(SUBLANES.bit_length() - 1, MOE_TILE.bit_length() - 1)):
                bit = (rest >> b) & 1

                @pl.when(bit == 1)
                def _(off=off, size=1 << b):
                    act(zero_copy(pl.multiple_of(off, SUBLANES), size))
                off = off + bit * (1 << b)

        def for_all(act):
            def per_expert(e, carry):
                pad_pieces(e, act)
                return carry

            def per_tail_tile(t, carry):
                act(zero_copy(pl.multiple_of(t * MOE_TILE, MOE_TILE), MOE_TILE))
                return carry
            lax.fori_loop(0, n_experts, per_expert, 0)
            lax.fori_loop(nu_ref[0], n_tiles, per_tail_tile, 0)

        for_all(lambda cp: cp.start())
        for_all(lambda cp: cp.wait())

    def issue(r, carry):
        for k in range(TOP_K):
            dst = pos_ref[0, r * TOP_K + k]
            pltpu.make_async_copy(f_ref.at[pl.ds(r, 1)], xb_ref.at[pl.ds(dst, 1)], sem).start()
        return carry

    lax.fori_loop(0, tt, issue, 0)
    for k in range(TOP_K):
        pltpu.make_async_copy(f_ref, f_ref, sem).wait()


def _dispatch(f_packed, pos, fill_start, fill_len, n_used, n_tiles):
    N, half = f_packed.shape
    tt = min(512, N)
    n_rows = n_tiles * MOE_TILE
    grid_spec = pltpu.PrefetchScalarGridSpec(
        num_scalar_prefetch=3,
        grid=(N // tt,),
        in_specs=[pl.BlockSpec((None, 1, tt * TOP_K), lambda i, fs, fl, nu: (i, 0, 0), memory_space=pltpu.SMEM),
                  pl.BlockSpec((tt, half), lambda i, fs, fl, nu: (i, 0))],
        out_specs=pl.BlockSpec(memory_space=pl.ANY),
        scratch_shapes=[pltpu.VMEM((MOE_TILE, half), f_packed.dtype),
                        pltpu.SemaphoreType.DMA, pltpu.SemaphoreType.DMA],
    )
    return pl.pallas_call(
        functools.partial(_dispatch_kernel, tt=tt, n_experts=fill_len.shape[0], n_tiles=n_tiles),
        out_shape=jax.ShapeDtypeStruct((n_rows, half), f_packed.dtype),
        grid_spec=grid_spec,
        compiler_params=_params("arbitrary"),
        name="moe_dispatch",
    )(fill_start, fill_len, n_used, pos.reshape(N // tt, 1, tt * TOP_K), f_packed)


def _combine_kernel(pos_ref, posn_ref, gate_ref, x_ref, g2_ref, *rest, tt, n_steps, final, ctx_steps):
    if final:
        ng_ref, yb_ref, octx_ref, olat_ref, buf, sem, keep = rest
    else:
        yb_ref, o_ref, buf, sem = rest
    i = pl.program_id(0)
    slot = i % 2

    def gather(p_ref, sl):
        def issue(r, carry):
            for k in range(TOP_K):
                src = p_ref[0, r * TOP_K + k]
                pltpu.make_async_copy(yb_ref.at[pl.ds(src, 1)], buf.at[sl, k, pl.ds(r, 1)], sem.at[sl]).start()
            return carry
        lax.fori_loop(0, tt, issue, 0)

    @pl.when(i == 0)
    def _():
        gather(pos_ref, 0)

    @pl.when(i + 1 < n_steps)
    def _():
        gather(posn_ref, 1 - slot)

    for k in range(TOP_K):
        pltpu.make_async_copy(buf.at[slot, k], buf.at[slot, k], sem.at[slot]).wait()
    gate = gate_ref[...]
    y = gate[:, 0:1] * buf[slot, 0]
    for k in range(1, TOP_K):
        y = y + gate[:, k:k + 1] * buf[slot, k]
    xn = x_ref[...] + g2_ref[...] * y
    if final:
        xn = xn * lax.rsqrt(jnp.mean(xn * xn, axis=-1, keepdims=True) + EPS) * ng_ref[...]

        @pl.when(i < ctx_steps)
        def _():
            octx_ref[...] = xn
            keep[...] = xn
            olat_ref[...] = jnp.zeros(olat_ref.shape, olat_ref.dtype)

        @pl.when(i >= ctx_steps)
        def _():
            olat_ref[...] = xn
            octx_ref[...] = keep[...]
    else:
        o_ref[...] = xn


def _combine(x, yb, pos, top_gate, mods, layer, final_g):
    G, T, D = x.shape
    N = G * T
    tt = min(256, T)
    per_b = T // tt
    n_steps = N // tt
    final = final_g is not None
    pos3 = pos.reshape(n_steps, 1, tt * TOP_K)
    in_specs = [
        pl.BlockSpec((None, 1, tt * TOP_K), lambda i: (i, 0, 0), memory_space=pltpu.SMEM),
        pl.BlockSpec((None, 1, tt * TOP_K), lambda i: (jnp.minimum(i + 1, n_steps - 1), 0, 0),
                     memory_space=pltpu.SMEM),
        pl.BlockSpec((tt, LANES), lambda i: (i, 0)),
        pl.BlockSpec((tt, D), lambda i: (i, 0)),
        pl.BlockSpec((None, None, None, 1, D), lambda i: (layer, 5, i // per_b, 0, 0)),
    ]
    args = [pos3, pos3, top_gate, x.reshape(N, D), mods]
    if final:
        in_specs.append(pl.BlockSpec((1, D), lambda i: (0, 0)))
        args.append(final_g.reshape(1, D))
    in_specs.append(pl.BlockSpec(memory_space=pl.ANY))
    args.append(yb)
    scratch = [pltpu.VMEM((2, TOP_K, tt, D), jnp.float32), pltpu.SemaphoreType.DMA((2,))]
    if final:
        out_shape = [jax.ShapeDtypeStruct((T, D), jnp.float32), jax.ShapeDtypeStruct((N - T, D), jnp.float32)]
        out_specs = [pl.BlockSpec((tt, D), lambda i: (jnp.minimum(i, per_b - 1), 0)),
                     pl.BlockSpec((tt, D), lambda i: (jnp.maximum(i - per_b, 0), 0))]
        scratch.append(pltpu.VMEM((tt, D), jnp.float32))
    else:
        out_shape = jax.ShapeDtypeStruct((N, D), jnp.float32)
        out_specs = pl.BlockSpec((tt, D), lambda i: (i, 0))
    out = pl.pallas_call(
        functools.partial(_combine_kernel, tt=tt, n_steps=n_steps, final=final, ctx_steps=per_b),
        out_shape=out_shape,
        grid=(n_steps,),
        in_specs=in_specs,
        out_specs=out_specs,
        scratch_shapes=scratch,
        compiler_params=_params("arbitrary"),
        name="moe_combine",
    )(*args)
    if final:
        return out[0], out[1].reshape(G - 1, T, D)
    return out.reshape(G, T, D)


def _moe_layer(x, f_packed, top_i, top_g, cnt, mods, layer, w_gate_up, b_gate_up, w_down, b_down, final_g):
    G, T, D = x.shape
    N = G * T
    E = w_gate_up.shape[1]
    counts = cnt[0, :E].astype(jnp.int32)
    padded = (counts + MOE_TILE - 1) // MOE_TILE * MOE_TILE
    pad_end = jnp.cumsum(padded)
    pad_start = pad_end - padded
    ti = top_i.reshape(N, LANES)
    top_e, rank = ti[:, :TOP_K], ti[:, TOP_K:2 * TOP_K]
    start_of = jnp.sum(jnp.where(top_e[:, :, None] == jnp.arange(E, dtype=jnp.int32), pad_start, 0), axis=-1)
    pos = (start_of + rank).astype(jnp.int32)
    n_tiles = (N * TOP_K + MOE_TILE - 1) // MOE_TILE + E
    tile_row0 = jnp.arange(n_tiles, dtype=jnp.int32) * MOE_TILE
    tile_expert = jnp.minimum(jnp.searchsorted(pad_end, tile_row0, side="right"), E - 1).astype(jnp.int32)
    valid = jnp.clip(counts[tile_expert] - (tile_row0 - pad_start[tile_expert]), 0, MOE_TILE)
    valid = jnp.where(tile_row0 < pad_end[-1], valid, 0)
    tile_nsub = ((valid + MOE_SUB - 1) // MOE_SUB).astype(jnp.int32)
    n_used = (pad_end[-1:] // MOE_TILE).astype(jnp.int32)
    xb = _dispatch(f_packed.reshape(N, D // 2), pos, (pad_start + counts).astype(jnp.int32),
                   (padded - counts).astype(jnp.int32), n_used, n_tiles)
    yb = _moe_experts(xb, tile_expert, tile_nsub, w_gate_up, b_gate_up, w_down, b_down, layer)
    return _combine(x, yb, pos, top_g.reshape(N, LANES), mods, layer, final_g)


def _rope_tables(T):
    n_rows = T // GRID_W
    row = jnp.repeat(jnp.arange(n_rows, dtype=jnp.float32), GRID_W)
    col = jnp.tile(jnp.arange(GRID_W, dtype=jnp.float32), n_rows)
    inv = ROPE_THETA ** (-jnp.arange(0, AXIS_DIM, 2, dtype=jnp.float32) / AXIS_DIM)
    cr, sr = jnp.cos(row[:, None] * inv), jnp.sin(row[:, None] * inv)
    cc, sc = jnp.cos(col[:, None] * inv), jnp.sin(col[:, None] * inv)
    cos_t = jnp.concatenate([cr, cr, cc, cc], axis=-1)
    sin_t = jnp.concatenate([-sr, sr, -sc, sc], axis=-1)
    return (jnp.stack([jnp.ones_like(cos_t), cos_t]), jnp.stack([jnp.zeros_like(sin_t), sin_t]))


def _heads_out(p0, col, n, d, Bp, Tp):
    return p0[:, col:col + n * d].reshape(Bp, Tp, n, d).transpose(0, 2, 1, 3)


def kernel(x_prompt, x_sample, cache_a_k, cache_a_v, cache_b_k, cache_b_v, cache_c_k, cache_c_v, state_d_C, state_d_n, state_d_m, c, c_ctx, norm_mix_g, norm_ffn_g, w_mod, b_mod, w_out, w_in_ab, a_q_norm_g, a_k_norm_g, b_lambda, b_subln_g, w_in_cd, b_gates, c_sink, d_norm_g, router_w, router_b, w_gate_up, b_gate_up, w_down, b_down, final_norm_g):
    Bp, Tp, D = x_prompt.shape
    Bs, T, _ = x_sample.shape
    assert Bp * Tp == T, "context tokens are laid out as one extra batch of DEC_SEQ tokens"
    depth = w_mod.shape[0]
    E = router_w.shape[-1]
    past = cache_a_k.shape[-2]
    G = 1 + Bs
    scale = HEAD_DIM ** -0.5 * LOG2E
    f32 = jnp.float32

    x = jnp.concatenate([x_prompt.reshape(1, T, D), x_sample], axis=0)
    cond = jnp.concatenate([c_ctx[None], c, jnp.zeros((8 - G, D), f32)], axis=0)
    mods = _modulation(cond, w_mod, b_mod)
    mods = mods[:, :G].reshape(depth, G, 6, 1, D).transpose(0, 2, 1, 3, 4)

    cos_t, sin_t = _rope_tables(T)
    pad_e = LANES - E
    rw = jnp.pad(router_w, ((0, 0), (0, 0), (0, pad_e)))
    rwh = rw.astype(_MXU)
    rwl = (rw - rwh.astype(f32)).astype(_MXU)
    rb = jnp.pad(router_b, ((0, 0), (0, pad_e)), constant_values=NEG_INF).reshape(depth, 1, LANES)

    ab_ops = ([("q", True, scale)] * A_HEADS + [("k", True, 1.0)] * A_KV_HEADS + [(None, False, 1.0)] * A_KV_HEADS
              + [(None, True, scale)] * (2 * B_HEADS) + [(None, True, 1.0)] * (2 * B_HEADS)
              + [(None, False, 1.0)] * (B_HEADS * B_VDIM // HEAD_DIM))
    cd_ops = ([(None, True, scale)] * C_HEADS + [(None, True, 1.0)] * C_KV_HEADS + [(None, False, 1.0)] * C_KV_HEADS
              + [(None, False, 1.0)] * D_HEADS + [(None, False, D_QK ** -0.5)] * D_HEADS
              + [(None, False, 1.0)] * (2 * D_HEADS * D_V // HEAD_DIM))
    A_Q, A_K, A_V = 0, A_HEADS * HEAD_DIM, (A_HEADS + A_KV_HEADS) * HEAD_DIM
    B_Q = (A_HEADS + 2 * A_KV_HEADS) * HEAD_DIM
    B_K = B_Q + 2 * B_HEADS * HEAD_DIM
    B_V = B_K + 2 * B_HEADS * HEAD_DIM
    C_Q, C_K, C_V = 0, C_HEADS * HEAD_DIM, (C_HEADS + C_KV_HEADS) * HEAD_DIM
    D_Q = (C_HEADS + 2 * C_KV_HEADS) * HEAD_DIM
    D_K = D_Q + D_HEADS * D_QK
    D_VC = D_K + D_HEADS * D_QK
    D_O = D_VC + D_HEADS * D_V
    a_grp = A_HEADS // A_KV_HEADS
    c_grp = C_HEADS // C_KV_HEADS
    ones_g = jnp.ones((1, HEAD_DIM), f32)

    new = {k: [] for k in ("ak", "av", "bk", "bv", "ck", "cv", "dC", "dn", "dm")}

    for layer in range(depth):
        jl = layer // 2
        if layer % 2 == 0:
            lam_init = 0.8 - 0.6 * math.exp(-0.3 * layer)
            p = _project(x, mods, layer, norm_mix_g, w_in_ab, jl, AB_WIDTH, ab_ops, cos_t, sin_t,
                         a_q_norm_g[jl].reshape(1, HEAD_DIM), a_k_norm_g[jl].reshape(1, HEAD_DIM))
            p0 = p[0]
            new["ak"].append(_heads_out(p0, A_K, A_KV_HEADS, HEAD_DIM, Bp, Tp))
            new["av"].append(_heads_out(p0, A_V, A_KV_HEADS, HEAD_DIM, Bp, Tp))
            new["bk"].append(p0[:, B_K:B_V].reshape(Bp, Tp, B_HEADS, 2, HEAD_DIM).transpose(0, 3, 2, 1, 4))
            new["bv"].append(_heads_out(p0, B_V, B_HEADS, B_VDIM, Bp, Tp))

            o = jnp.zeros((G, T, MIX_WIDTH), _MXU)
            a_groups = ((tuple(t * HEAD_DIM for t in range(a_grp)), 0),)
            b_groups = (((0,), 0), ((HEAD_DIM,), HEAD_DIM))
            diff = (b_lambda[jl], b_subln_g[jl].reshape(1, B_VDIM), lam_init)
            a_common = dict(n_kvh=A_KV_HEADS, groups=a_groups, q_col=A_Q, q_w=a_grp * HEAD_DIM, k_col=A_K,
                            k_w=HEAD_DIM, v_col=A_V, dv=HEAD_DIM, o_col=0, o_w=a_grp * HEAD_DIM)
            b_common = dict(n_kvh=B_HEADS, groups=b_groups, q_col=B_Q, q_w=2 * HEAD_DIM, k_col=B_K,
                            k_w=2 * HEAD_DIM, v_col=B_V, dv=B_VDIM, o_col=A_HEADS * HEAD_DIM, o_w=B_VDIM, diff=diff)
            o = _attention(p, o, n_seq=Bp, seq_len=Tp, b_off=0, tq=min(256, Tp), name="ctx_attn_a", **a_common)
            o = _attention(p, o, n_seq=Bp, seq_len=Tp, b_off=0, tq=min(256, Tp), name="ctx_attn_b", **b_common)
            cache_a = ([(cache_a_k, pl.BlockSpec((None, None, None, past, HEAD_DIM),
                                                  lambda s, h, i: (s, jl, h, 0, 0)))],
                       (cache_a_v, pl.BlockSpec((None, None, None, past, HEAD_DIM),
                                                lambda s, h, i: (s, jl, h, 0, 0))))
            cache_b = ([(cache_b_k, pl.BlockSpec((None, None, None, None, past, HEAD_DIM),
                                                  lambda s, h, i, mm=mm: (s, jl, mm, h, 0, 0))) for mm in range(2)],
                       (cache_b_v, pl.BlockSpec((None, None, None, past, B_VDIM),
                                                lambda s, h, i: (s, jl, h, 0, 0))))
            o = _attention(p, o, n_seq=Bs, seq_len=T, b_off=1, tq=min(512, T), cache=cache_a,
                           name="latent_attn_a", **a_common)
            o = _attention(p, o, n_seq=Bs, seq_len=T, b_off=1, tq=min(1024, T), cache=cache_b,
                           name="latent_attn_b", **b_common)
            x, f_tok, top_i, top_g, cnt = _out_project(x, mods, layer, w_out, norm_ffn_g, rwh, rwl, rb, E, o=o)
        else:
            wg = jnp.pad(w_in_cd[jl][:, CD_MAIN:], ((0, 0), (0, LANES - N_GATES)))
            bg = jnp.pad(b_gates[jl], (0, LANES - N_GATES)).reshape(1, LANES)
            p, gates = _project(x, mods, layer, norm_mix_g, w_in_cd, jl, CD_MAIN, cd_ops, cos_t, sin_t,
                                ones_g, ones_g, gates=(wg, bg))
            gates_t = jnp.swapaxes(gates[:, :, :N_GATES], 1, 2)
            p0 = p[0]
            new["ck"].append(_heads_out(p0, C_K, C_KV_HEADS, HEAD_DIM, Bp, Tp))
            new["cv"].append(_heads_out(p0, C_V, C_KV_HEADS, HEAD_DIM, Bp, Tp))

            oc = jnp.zeros((G, T, C_HEADS * HEAD_DIM), _MXU)
            c_groups = ((tuple(t * HEAD_DIM for t in range(c_grp)), 0),)
            oc = _attention(p, oc, n_seq=Bp, seq_len=Tp, b_off=0, tq=min(256, Tp), n_kvh=C_KV_HEADS,
                            groups=c_groups, q_col=C_Q, q_w=c_grp * HEAD_DIM, k_col=C_K, k_w=HEAD_DIM,
                            v_col=C_V, dv=HEAD_DIM, o_col=0, o_w=c_grp * HEAD_DIM, sink=c_sink[jl],
                            name="ctx_attn_c")
            oc = _window_attention(p, oc, cache_c_k, cache_c_v, jl, c_sink[jl], b_off=1, n_b=Bs,
                                   q_col=C_Q, k_col=C_K, v_col=C_V)
            hf = jnp.zeros((G, T, D_HEADS * D_V), f32)
            hb = jnp.zeros((G, T, D_HEADS * D_V), f32)
            hf, hb, dC, dn, dm = _mlstm(p, gates, gates_t, hf, hb, n_seq=Bp, seq_len=Tp, b_off=0,
                                        q_col=D_Q, k_col=D_K, v_col=D_VC, emit_state=True)
            new["dC"].append(dC.reshape(Bp, 2, D_HEADS, D_QK, D_V))
            new["dn"].append(dn.reshape(Bp, 2, D_HEADS, D_QK))
            new["dm"].append(dm[:, :, 0].reshape(Bp, 2, D_HEADS))
            n_odd = state_d_C.shape[1]
            init = (state_d_C.reshape(Bs, n_odd, 2 * D_HEADS, D_QK, D_V),
                    state_d_n.reshape(Bs, n_odd, 2 * D_HEADS, D_QK),
                    jnp.broadcast_to(state_d_m.reshape(Bs, n_odd, 2 * D_HEADS, 1), (Bs, n_odd, 2 * D_HEADS, LANES)),
                    jl)
            hf, hb = _mlstm(p, gates, gates_t, hf, hb, n_seq=Bs, seq_len=T, b_off=1,
                            q_col=D_Q, k_col=D_K, v_col=D_VC, init=init)
            x, f_tok, top_i, top_g, cnt = _out_project(
                x, mods, layer, w_out, norm_ffn_g, rwh, rwl, rb, E,
                odd_in=(oc, hf, hb, p, D_O, d_norm_g[jl].reshape(1, D_HEADS * D_V)))

        x = _moe_layer(x, f_tok, top_i, top_g, cnt, mods, layer, w_gate_up, b_gate_up, w_down, b_down,
                       final_norm_g if layer == depth - 1 else None)

    y_ctx, y_sample = x
    y_prompt = y_ctx.reshape(Bp, Tp, D)
    return (y_prompt, y_sample,
            jnp.stack(new["ak"], 1), jnp.stack(new["av"], 1), jnp.stack(new["bk"], 1), jnp.stack(new["bv"], 1),
            jnp.stack(new["ck"], 1), jnp.stack(new["cv"], 1), jnp.stack(new["dC"], 1), jnp.stack(new["dn"], 1),
            jnp.stack(new["dm"], 1))
```
